```python
import math
import jax
import jax.numpy as jnp
from jax import lax
import numpy as np

D_MODEL = 1024
BATCH = 2
SEQ = 16384
DEPTH = 2
DEC_BATCH = 8
DEC_SEQ = 4096
PAST_LEN = 128

GRID_W = 64
HEAD_DIM = 64
Q_BLOCK = 128
NORM_EPS = 1e-6

A_HEADS = 6
A_KV_HEADS = 2
ROPE_THETA = 10000.0

B_HEADS = 4
B_QK_DIM = 32
B_V_DIM = 2 * B_QK_DIM

C_HEADS = 4
C_K_DIM = 48
C_V_DIM = 96
C_GATE_RANK = 16
C_GATE_TAU = 16.0
C_CHUNK = 64

A_WIDTH = A_HEADS * HEAD_DIM
B_WIDTH = B_HEADS * B_V_DIM
C_WIDTH = C_HEADS * C_V_DIM
MIX_WIDTH = A_WIDTH + B_WIDTH + C_WIDTH
IN_SIZES = (A_HEADS * HEAD_DIM, A_KV_HEADS * HEAD_DIM, A_KV_HEADS * HEAD_DIM,
            B_HEADS * 2 * B_QK_DIM, B_HEADS * 2 * B_QK_DIM, B_HEADS * B_V_DIM,
            C_HEADS * C_K_DIM, C_HEADS * C_K_DIM, C_HEADS * C_V_DIM,
            C_GATE_RANK, C_GATE_RANK, C_WIDTH)
IN_WIDTH = sum(IN_SIZES)

N_GROUPS = 4
EXPERTS_PER_GROUP = 8
N_EXPERTS = N_GROUPS * EXPERTS_PER_GROUP
TOP_K = 2
D_EXPERT = 512
MOE_BLOCK = 128

PLE_DIM = 256
DEEPNORM_ALPHA = (2.0 * DEPTH) ** 0.25
DEEPNORM_BETA = (8.0 * DEPTH) ** -0.25

kernel_name = 'hybrid_parallel_heads_encoder'


def rms_norm(x, g):
    xf = x.astype(jnp.float32)
    y = xf * lax.rsqrt(jnp.mean(xf * xf, axis=-1, keepdims=True) + NORM_EPS)
    return (y * g.astype(jnp.float32)).astype(x.dtype)


def layer_norm(x, g, b):
    xf = x.astype(jnp.float32)
    xc = xf - jnp.mean(xf, axis=-1, keepdims=True)
    var = jnp.mean(xc * xc, axis=-1, keepdims=True)
    return (xc * lax.rsqrt(var + NORM_EPS) * g.astype(jnp.float32) + b.astype(jnp.float32)).astype(x.dtype)


def split_heads(t, n_heads):
    bsz, seq, _ = t.shape
    return t.reshape(bsz, seq, n_heads, -1).transpose(0, 2, 1, 3)


def merge_heads(t):
    bsz, h, seq, d = t.shape
    return t.transpose(0, 2, 1, 3).reshape(bsz, seq, h * d)


def axial_rope_tables(seq_len):
    rows = seq_len // GRID_W
    row = jnp.repeat(jnp.arange(rows, dtype=jnp.float32), GRID_W)
    col = jnp.tile(jnp.arange(GRID_W, dtype=jnp.float32), rows)
    n_pairs = HEAD_DIM // 4
    inv_freq = ROPE_THETA ** (-jnp.arange(n_pairs, dtype=jnp.float32) / n_pairs)
    ang = jnp.concatenate([row[:, None] * inv_freq, col[:, None] * inv_freq], axis=-1)
    return jnp.cos(ang), jnp.sin(ang)


def apply_rope(x, cos, sin):
    xf = x.astype(jnp.float32).reshape(x.shape[:-1] + (HEAD_DIM // 2, 2))
    x0, x1 = xf[..., 0], xf[..., 1]
    out = jnp.stack([x0 * cos - x1 * sin, x0 * sin + x1 * cos], axis=-1)
    return out.reshape(x.shape).astype(x.dtype)


def gqa_block_attention(q, k, v):
    bsz, _, seq, _ = q.shape
    groups = A_HEADS // A_KV_HEADS
    n_blocks = seq // Q_BLOCK
    scale = HEAD_DIM ** -0.5
    qb = q.reshape(bsz, A_KV_HEADS, groups, n_blocks, Q_BLOCK, HEAD_DIM).transpose(3, 0, 1, 2, 4, 5)

    def one_block(qi):
        s = jnp.einsum('bkgqd,bksd->bkgqs', qi, k).astype(jnp.float32) * scale
        p = jax.nn.softmax(s, axis=-1).astype(v.dtype)
        return jnp.einsum('bkgqs,bksd->bkgqd', p, v)

    o = lax.map(one_block, qb)
    return o.transpose(1, 2, 3, 0, 4, 5).reshape(bsz, A_HEADS, seq, HEAD_DIM)


def alibi_slopes(n_heads):
    return 2.0 ** (-8.0 * jnp.arange(1, n_heads + 1, dtype=jnp.float32) / n_heads)


def diff_alibi_block_attention(q, k, v, lam):
    bsz, _, _, seq, _ = q.shape
    n_blocks = seq // Q_BLOCK
    scale = B_QK_DIM ** -0.5
    slopes = alibi_slopes(B_HEADS)
    kpos = jnp.arange(seq, dtype=jnp.float32)
    qb = q.reshape(bsz, B_HEADS, 2, n_blocks, Q_BLOCK, B_QK_DIM).transpose(3, 0, 1, 2, 4, 5)
    starts = jnp.arange(n_blocks, dtype=jnp.float32) * Q_BLOCK

    def one_block(args):
        qi, start = args
        qpos = start + jnp.arange(Q_BLOCK, dtype=jnp.float32)
        bias = -slopes[:, None, None] * jnp.abs(qpos[:, None] - kpos[None, :])
        s = jnp.einsum('bhcqd,bhcsd->bhcqs', qi, k).astype(jnp.float32) * scale + bias[None, :, None]
        p = jax.nn.softmax(s, axis=-1)
        w = (p[:, :, 0] - lam * p[:, :, 1]).astype(v.dtype)
        return jnp.einsum('bhqs,bhsd->bhqd', w, v)

    o = lax.map(one_block, (qb, starts))
    return o.transpose(1, 2, 0, 3, 4).reshape(bsz, B_HEADS, seq, B_V_DIM)


def gla_chunked_scan(q, k, v, log_a):
    bsz, n_heads, seq, dk = q.shape
    dv = v.shape[-1]
    n_chunks = seq // C_CHUNK

    def to_chunks(t):
        return t.reshape(bsz, n_heads, n_chunks, C_CHUNK, t.shape[-1]).transpose(2, 0, 1, 3, 4)

    lower = jnp.tril(jnp.ones((C_CHUNK, C_CHUNK), dtype=bool))[:, :, None]

    def step(state, inp):
        qc, kc, vc, gc = inp
        b = jnp.cumsum(gc, axis=2)
        b_last = b[:, :, -1:, :]
        o_inter = jnp.einsum('bhcd,bhde->bhce', qc * jnp.exp(b), state)
        rel = b[:, :, :, None, :] - b[:, :, None, :, :]
        decay = jnp.where(lower, jnp.exp(jnp.minimum(rel, 0.0)), 0.0)
        scores = jnp.einsum('bhid,bhjd,bhijd->bhij', qc, kc, decay)
        o_intra = jnp.einsum('bhij,bhje->bhie', scores, vc)
        new_state = state * jnp.exp(b_last)[:, :, 0, :, None] + jnp.einsum('bhcd,bhce->bhde', kc * jnp.exp(b_last - b), vc)
        return new_state, o_inter + o_intra

    state0 = jnp.zeros((bsz, n_heads, dk, dv), jnp.float32)
    _, o = lax.scan(step, state0, (to_chunks(q), to_chunks(k), to_chunks(v), to_chunks(log_a)))
    return o.transpose(1, 2, 0, 3, 4).reshape(bsz, n_heads, seq, dv)


def expert_ffn_sorted(xf, expert_id, gate, w_gate, w_up, w_down):
    n_tok, d = xf.shape
    n_assign = n_tok * TOP_K
    flat_e = expert_id.reshape(-1)
    flat_tok = jnp.repeat(jnp.arange(n_tok, dtype=jnp.int32), TOP_K)
    flat_g = gate.reshape(-1)
    order = jnp.argsort(flat_e)
    sorted_e = flat_e[order]
    counts = jnp.bincount(flat_e, length=N_EXPERTS)
    seg_start = jnp.cumsum(counts) - counts
    padded = (counts + MOE_BLOCK - 1) // MOE_BLOCK * MOE_BLOCK
    pad_end = jnp.cumsum(padded)
    pad_start = pad_end - padded
    dest = pad_start[sorted_e] + jnp.arange(n_assign, dtype=jnp.int32) - seg_start[sorted_e]
    n_blocks = -(-n_assign // MOE_BLOCK) + N_EXPERTS
    n_slots = n_blocks * MOE_BLOCK
    slot_tok = jnp.full((n_slots,), n_tok, jnp.int32).at[dest].set(flat_tok[order])
    slot_gate = jnp.zeros((n_slots,), xf.dtype).at[dest].set(flat_g[order].astype(xf.dtype))
    block_expert = jnp.minimum(jnp.searchsorted(pad_end, jnp.arange(n_blocks, dtype=jnp.int32) * MOE_BLOCK, side='right'), N_EXPERTS - 1)
    x_pad = jnp.concatenate([xf, jnp.zeros((1, d), xf.dtype)], axis=0)
    xs = x_pad[slot_tok].reshape(n_blocks, MOE_BLOCK, d)

    def one_block(args):
        xb, e = args
        h = jax.nn.silu(xb @ w_gate[e]) * (xb @ w_up[e])
        return h @ w_down[e]

    ys = lax.map(one_block, (xs, block_expert)).reshape(n_slots, d)
    return jax.ops.segment_sum(ys * slot_gate[:, None], slot_tok, num_segments=n_tok + 1)[:n_tok]


def hierarchical_moe(x, w_coarse, b_coarse, w_fine, b_fine, w_gate, w_up, w_down):
    bsz, seq, d = x.shape
    xf = x.reshape(bsz * seq, d)
    coarse = (xf @ w_coarse).astype(jnp.float32) + b_coarse.astype(jnp.float32)
    group = jnp.argmax(coarse, axis=-1).astype(jnp.int32)
    group_w = jnp.take_along_axis(jax.nn.softmax(coarse, axis=-1), group[:, None], axis=-1)
    fine = ((xf @ w_fine).astype(jnp.float32) + b_fine.astype(jnp.float32)).reshape(-1, N_GROUPS, EXPERTS_PER_GROUP)
    fine_sel = jnp.take_along_axis(fine, group[:, None, None], axis=1)[:, 0]
    top_val, top_idx = lax.top_k(fine_sel, TOP_K)
    gate = group_w * jax.nn.softmax(top_val, axis=-1)
    expert_id = group[:, None] * EXPERTS_PER_GROUP + top_idx.astype(jnp.int32)
    y = expert_ffn_sorted(xf, expert_id, gate, w_gate, w_up, w_down)
    return y.reshape(bsz, seq, d)


def encoder_layer(x, p_layer, i, w_in, a_q_norm, a_k_norm, b_lambda, b_subln, c_gate_w2, c_gate_b, c_norm,
                  w_out, ln1_g, ln1_b, w_router_coarse, b_router_coarse, w_router_fine, b_router_fine,
                  w_exp_gate, w_exp_up, w_exp_down, w_ple_gate, w_ple_proj, ln2_g, ln2_b):
    bsz, seq, _ = x.shape
    z = x @ w_in[i]
    parts = []
    off = 0
    for size in IN_SIZES:
        parts.append(z[..., off:off + size])
        off += size
    aq, ak, av, bq, bk, bv, cq, ck, cv, c_lr_f, c_lr_b, c_og = parts

    cos, sin = axial_rope_tables(seq)
    aq_h = apply_rope(rms_norm(split_heads(aq, A_HEADS), a_q_norm[i]), cos, sin)
    ak_h = apply_rope(rms_norm(split_heads(ak, A_KV_HEADS), a_k_norm[i]), cos, sin)
    o_a = merge_heads(gqa_block_attention(aq_h, ak_h, split_heads(av, A_KV_HEADS)))

    lam_init = 0.8 - 0.6 * math.exp(-0.3 * i)
    lam_vecs = b_lambda[i].astype(jnp.float32)
    lam = jnp.exp(jnp.sum(lam_vecs[0] * lam_vecs[1])) - jnp.exp(jnp.sum(lam_vecs[2] * lam_vecs[3])) + lam_init
    bq_h = bq.reshape(bsz, seq, B_HEADS, 2, B_QK_DIM).transpose(0, 2, 3, 1, 4)
    bk_h = bk.reshape(bsz, seq, B_HEADS, 2, B_QK_DIM).transpose(0, 2, 3, 1, 4)
    o_b = diff_alibi_block_attention(bq_h, bk_h, split_heads(bv, B_HEADS), lam)
    o_b = merge_heads(rms_norm(o_b, b_subln[i]) * (1.0 - lam_init))

    def log_decay(lowrank, direction):
        g = (lowrank @ c_gate_w2[i, direction]).astype(jnp.float32) + c_gate_b[i, direction].astype(jnp.float32)
        return split_heads(jax.nn.log_sigmoid(g) / C_GATE_TAU, C_HEADS)

    cq_h = split_heads(cq, C_HEADS).astype(jnp.float32) * (C_K_DIM ** -0.5)
    ck_h = split_heads(ck, C_HEADS).astype(jnp.float32)
    cv_h = split_heads(cv, C_HEADS).astype(jnp.float32)
    o_fwd = gla_chunked_scan(cq_h, ck_h, cv_h, log_decay(c_lr_f, 0))
    o_bwd = jnp.flip(gla_chunked_scan(jnp.flip(cq_h, axis=2), jnp.flip(ck_h, axis=2), jnp.flip(cv_h, axis=2),
                                      jnp.flip(log_decay(c_lr_b, 1), axis=2)), axis=2)
    o_c = merge_heads(rms_norm((o_fwd + o_bwd).astype(x.dtype), c_norm[i])) * jax.nn.silu(c_og)

    mixed = jnp.concatenate([o_a, o_b, o_c], axis=-1) @ w_out[i]
    x = layer_norm(DEEPNORM_ALPHA * x + mixed, ln1_g[i], ln1_b[i])

    ffn = hierarchical_moe(x, w_router_coarse[i], b_router_coarse[i], w_router_fine[i], b_router_fine[i],
                           w_exp_gate[i], w_exp_up[i], w_exp_down[i])
    ple = jax.nn.sigmoid(x @ w_ple_gate[i]) * (p_layer @ w_ple_proj[i])
    return layer_norm(DEEPNORM_ALPHA * x + ffn + ple, ln2_g[i], ln2_b[i])


def setup_inputs(seed: int = 0) -> dict:
    key = jax.random.key(seed)
    ks = jax.random.split(key, 26)

    def nrm(k, shape, s):
        return jax.random.normal(k, shape, jnp.float32) * s

    return {
        'x_prompt': nrm(ks[0], (BATCH, SEQ, D_MODEL), 1.0),
        'x_sample': nrm(ks[1], (DEC_BATCH, DEC_SEQ, D_MODEL), 1.0),
        'p_prompt': nrm(ks[2], (DEPTH, BATCH, SEQ, PLE_DIM), 1.0),
        'p_sample': nrm(ks[3], (DEPTH, DEC_BATCH, DEC_SEQ, PLE_DIM), 1.0),
        'w_in': nrm(ks[4], (DEPTH, D_MODEL, IN_WIDTH), D_MODEL ** -0.5),
        'a_q_norm': 1.0 + nrm(ks[5], (DEPTH, HEAD_DIM), 0.02),
        'a_k_norm': 1.0 + nrm(ks[6], (DEPTH, HEAD_DIM), 0.02),
        'b_lambda': nrm(ks[7], (DEPTH, 4, B_QK_DIM), 0.1),
        'b_subln': 1.0 + nrm(ks[8], (DEPTH, B_V_DIM), 0.02),
        'c_gate_w2': nrm(ks[9], (DEPTH, 2, C_GATE_RANK, C_HEADS * C_K_DIM), C_GATE_RANK ** -0.5),
        'c_gate_b': nrm(ks[10], (DEPTH, 2, C_HEADS * C_K_DIM), 0.1),
        'c_norm': 1.0 + nrm(ks[11], (DEPTH, C_V_DIM), 0.02),
        'w_out': nrm(ks[12], (DEPTH, MIX_WIDTH, D_MODEL), MIX_WIDTH ** -0.5 * DEEPNORM_BETA),
        'ln1_g': 1.0 + nrm(ks[13], (DEPTH, D_MODEL), 0.02),
        'ln1_b': nrm(ks[14], (DEPTH, D_MODEL), 0.02),
        'w_router_coarse': nrm(ks[15], (DEPTH, D_MODEL, N_GROUPS), D_MODEL ** -0.5),
        'b_router_coarse': nrm(ks[16], (DEPTH, N_GROUPS), 0.01),
        'w_router_fine': nrm(ks[17], (DEPTH, D_MODEL, N_EXPERTS), D_MODEL ** -0.5),
        'b_router_fine': nrm(ks[18], (DEPTH, N_EXPERTS), 0.01),
        'w_exp_gate': nrm(ks[19], (DEPTH, N_EXPERTS, D_MODEL, D_EXPERT), D_MODEL ** -0.5),
        'w_exp_up': nrm(ks[20], (DEPTH, N_EXPERTS, D_MODEL, D_EXPERT), D_MODEL ** -0.5),
        'w_exp_down': nrm(ks[21], (DEPTH, N_EXPERTS, D_EXPERT, D_MODEL), D_EXPERT ** -0.5 * DEEPNORM_BETA),
        'w_ple_gate': nrm(ks[22], (DEPTH, D_MODEL, D_MODEL), D_MODEL ** -0.5),
        'w_ple_proj': nrm(ks[23], (DEPTH, PLE_DIM, D_MODEL), PLE_DIM ** -0.5 * DEEPNORM_BETA),
        'ln2_g': 1.0 + nrm(ks[24], (DEPTH, D_MODEL), 0.02),
        'ln2_b': nrm(ks[25], (DEPTH, D_MODEL), 0.02),
    }


def reference(x_prompt, x_sample, p_prompt, p_sample, w_in, a_q_norm, a_k_norm, b_lambda, b_subln, c_gate_w2,
              c_gate_b, c_norm, w_out, ln1_g, ln1_b, w_router_coarse, b_router_coarse, w_router_fine,
              b_router_fine, w_exp_gate, w_exp_up, w_exp_down, w_ple_gate, w_ple_proj, ln2_g, ln2_b):
    def trunk(x, p):
        for i in range(DEPTH):
            x = encoder_layer(x, p[i], i, w_in=w_in, a_q_norm=a_q_norm, a_k_norm=a_k_norm, b_lambda=b_lambda,
                              b_subln=b_subln, c_gate_w2=c_gate_w2, c_gate_b=c_gate_b, c_norm=c_norm,
                              w_out=w_out, ln1_g=ln1_g, ln1_b=ln1_b, w_router_coarse=w_router_coarse,
                              b_router_coarse=b_router_coarse, w_router_fine=w_router_fine,
                              b_router_fine=b_router_fine, w_exp_gate=w_exp_gate, w_exp_up=w_exp_up,
                              w_exp_down=w_exp_down, w_ple_gate=w_ple_gate, w_ple_proj=w_ple_proj,
                              ln2_g=ln2_g, ln2_b=ln2_b)
        return x

    y_prompt = trunk(x_prompt, p_prompt)
    y_sample = trunk(x_sample, p_sample)
    return (y_prompt, y_sample)
```

```python
import functools
import math

import numpy as np
import jax
import jax.numpy as jnp
from jax import lax
from jax.experimental import pallas as pl
from jax.experimental.pallas import tpu as pltpu

F32 = jnp.float32
BF16 = jnp.bfloat16

D_MODEL = 1024
GRID_W = 64
HEAD_DIM = 64
NORM_EPS = 1e-6
A_HEADS = 6
A_KV_HEADS = 2
A_GROUP = A_HEADS // A_KV_HEADS
ROPE_THETA = 10000.0
B_HEADS = 4
B_QK_DIM = 32
B_V_DIM = 64
B_MAPS = 2 * B_HEADS
C_HEADS = 4
C_K_DIM = 48
C_V_DIM = 96
C_GATE_RANK = 16
C_GATE_TAU = 16.0
C_CHUNK = 64
C_KW = C_HEADS * C_K_DIM
C_KP = 256
C_VW = C_HEADS * C_V_DIM
N_GROUPS = 4
EXPERTS_PER_GROUP = 8
N_EXPERTS = N_GROUPS * EXPERTS_PER_GROUP
D_EXPERT = 512
PLE_DIM = 256
A_QW = A_HEADS * HEAD_DIM
A_KW = A_KV_HEADS * HEAD_DIM
B_QW = B_HEADS * 2 * B_QK_DIM
B_VW = B_HEADS * B_V_DIM
ROUTE_ROWS = 8 + N_EXPERTS

VMEM_LIMIT = 56 * 1024 * 1024

ROW_TILE = 512
ATT_TQ = 512
ATT_TK = 512
GLA_BLOCK = 512
MOE_BM = 256


def _cparams(sem):
    return pltpu.CompilerParams(dimension_semantics=sem, vmem_limit_bytes=VMEM_LIMIT)


def _full(shape):
    nd = len(shape)
    return pl.BlockSpec(shape, lambda *_: (0,) * nd)


def _dot(a, b):
    return jnp.dot(a, b, preferred_element_type=F32)


def _dot_nt(a, b):
    return lax.dot_general(a, b, (((1,), (1,)), ((), ())), preferred_element_type=F32)


def _dot_tn(a, b):
    return lax.dot_general(a, b, (((0,), (0,)), ((), ())), preferred_element_type=F32)


def _split2(x):
    hi = x.astype(BF16)
    lo = (x - hi.astype(F32)).astype(BF16)
    return hi, lo


def _split3(x):
    hi = x.astype(BF16)
    r = x - hi.astype(F32)
    mid = r.astype(BF16)
    lo = (r - mid.astype(F32)).astype(BF16)
    return hi, mid, lo


def _seg_mean(sq, e):
    hi, lo = _split2(sq)
    return _dot(hi, e) + _dot(lo, e)


def _block_diag_avg(width, seg):
    idx = np.arange(width) // seg
    return jnp.asarray((idx[:, None] == idx[None, :]).astype(np.float32) / seg, dtype=BF16)


def _rope(x, cos, sin_signed):
    pieces = []
    for c in range(x.shape[1] // 128):
        xc = x[:, c * 128:(c + 1) * 128]
        nxt = pltpu.roll(xc, 127, 1)
        prv = pltpu.roll(xc, 1, 1)
        lane = lax.broadcasted_iota(jnp.int32, xc.shape, 1)
        sw = jnp.where(lane % 2 == 0, nxt, prv)
        pieces.append(xc * cos + sw * sin_signed)
    return pieces[0] if len(pieces) == 1 else jnp.concatenate(pieces, axis=1)


def _inproj_kernel(x_ref, wa_ref, wb_ref, wc_ref, w2f_ref, w2b_ref, gbf_ref, gbb_ref, gq_ref, gk_ref,
                   e64_ref, cos_ref, sin_ref,
                   qa_ref, ka_ref, va_ref, qb_ref, kb_ref, vb_ref, cq_ref, ck_ref, cv_ref, gf_ref, gb_ref,
                   og_ref):
    xb = x_ref[...].astype(BF16)
    cos = cos_ref[...]
    sin = sin_ref[...]

    za = _dot(xb, wa_ref[...])
    q = za[:, :A_QW]
    k = za[:, A_QW:A_QW + A_KW]
    e64 = e64_ref[...]
    qn = q * lax.rsqrt(_seg_mean(q * q, e64) + NORM_EPS) * gq_ref[...]
    kn = k * lax.rsqrt(_seg_mean(k * k, e64[:A_KW, :A_KW]) + NORM_EPS) * gk_ref[...]
    qa_ref[...] = (_rope(qn, cos, sin) * (HEAD_DIM ** -0.5)).astype(BF16)
    ka_ref[...] = _rope(kn, cos, sin).astype(BF16)
    va_ref[...] = za[:, A_QW + A_KW:].astype(BF16)

    zb = _dot(xb, wb_ref[...])
    qb_ref[...] = (zb[:, :B_QW] * (B_QK_DIM ** -0.5)).astype(BF16)
    kb_ref[...] = zb[:, B_QW:2 * B_QW].astype(BF16)
    vb_ref[...] = zb[:, 2 * B_QW:].astype(BF16)

    zc = _dot(xb, wc_ref[...])
    cq_ref[...] = zc[:, :C_KP] * (C_K_DIM ** -0.5)
    ck_ref[...] = zc[:, C_KP:2 * C_KP]
    cv_ref[...] = zc[:, 2 * C_KP:2 * C_KP + C_VW]
    og = zc[:, 2 * C_KP + C_VW:2 * C_KP + 2 * C_VW]
    og_ref[...] = og * (1.0 / (1.0 + jnp.exp(-og)))
    lr = zc[:, 2 * C_KP + 2 * C_VW:].astype(BF16)

    def log_decay(w2_ref, bias_ref):
        g = _dot(lr, w2_ref[...]) + bias_ref[...]
        return (jnp.minimum(g, 0.0) - jnp.log(1.0 + jnp.exp(-jnp.abs(g)))) * (1.0 / C_GATE_TAU)

    gf_ref[...] = log_decay(w2f_ref, gbf_ref)
    gb_ref[...] = log_decay(w2b_ref, gbb_ref)


def _inproj(x, wa, wb, wc, w2f, w2b, gbf, gbb, gq, gk, e64, cos_t, sin_t, pos_block):
    n = x.shape[0]
    tm = ROW_TILE
    row = lambda w: pl.BlockSpec((tm, w), lambda i: (i, 0))
    tab = pl.BlockSpec((tm, 128), lambda i: (pos_block(i), 0))
    outs = [(A_QW, BF16), (A_KW, BF16), (A_KW, BF16), (B_QW, BF16), (B_QW, BF16), (B_VW, BF16),
            (C_KP, F32), (C_KP, F32), (C_VW, F32), (C_KP, F32), (C_KP, F32), (C_VW, F32)]
    return pl.pallas_call(
        _inproj_kernel,
        grid=(n // tm,),
        in_specs=[row(D_MODEL), _full(wa.shape), _full(wb.shape), _full(wc.shape), _full(w2f.shape),
                  _full(w2b.shape), _full(gbf.shape), _full(gbb.shape), _full(gq.shape), _full(gk.shape),
                  _full(e64.shape), tab, tab],
        out_specs=[row(w) for w, _ in outs],
        out_shape=[jax.ShapeDtypeStruct((n, w), dt) for w, dt in outs],
        compiler_params=_cparams(("parallel",)),
        name="inproj",
    )(x, wa, wb, wc, w2f, w2b, gbf, gbb, gq, gk, e64, cos_t, sin_t)


def _online_softmax_step(s, vt, m_ref, l_ref, acc_ref, idx):
    m_prev = m_ref[idx]
    m_new = jnp.maximum(m_prev, jnp.max(s, axis=0, keepdims=True))
    alpha = jnp.exp(m_prev - m_new)
    p = jnp.exp(s - m_new)
    l_ref[idx] = alpha * l_ref[idx] + jnp.sum(p, axis=0, keepdims=True)
    acc_ref[idx] = alpha * acc_ref[idx] + _dot(vt, p.astype(BF16))
    m_ref[idx] = m_new


def _attn_init(m_ref, l_ref, acc_ref):
    m_ref[...] = jnp.full(m_ref.shape, -jnp.inf, F32)
    l_ref[...] = jnp.zeros(l_ref.shape, F32)
    acc_ref[...] = jnp.zeros(acc_ref.shape, F32)


def _attn_a_kernel(qt_ref, k_ref, vt_ref, ot_ref, qpad, m_ref, l_ref, acc_ref):
    ki = pl.program_id(2)

    @pl.when(ki == 0)
    def _():
        qpad[...] = jnp.zeros(qpad.shape, BF16)
        for h in range(A_HEADS):
            g = h // A_GROUP
            qpad[h, g * HEAD_DIM:(g + 1) * HEAD_DIM, :] = qt_ref[0, h * HEAD_DIM:(h + 1) * HEAD_DIM, :]
        _attn_init(m_ref, l_ref, acc_ref)

    k = k_ref[0]
    for h in range(A_HEADS):
        g = h // A_GROUP
        s = _dot(k, qpad[h])
        _online_softmax_step(s, vt_ref[0, g * HEAD_DIM:(g + 1) * HEAD_DIM, :], m_ref, l_ref, acc_ref, h)

    @pl.when(ki == pl.num_programs(2) - 1)
    def _():
        for h in range(A_HEADS):
            ot_ref[0, h * HEAD_DIM:(h + 1) * HEAD_DIM, :] = (acc_ref[h] / l_ref[h]).astype(BF16)


def _attn_a(qt, k, vt):
    bsz, _, seq = qt.shape
    tq, tk = min(ATT_TQ, seq), min(ATT_TK, seq)
    return pl.pallas_call(
        _attn_a_kernel,
        grid=(bsz, seq // tq, seq // tk),
        in_specs=[pl.BlockSpec((1, A_QW, tq), lambda b, qi, ki: (b, 0, qi)),
                  pl.BlockSpec((1, tk, A_KW), lambda b, qi, ki: (b, ki, 0)),
                  pl.BlockSpec((1, A_KW, tk), lambda b, qi, ki: (b, 0, ki))],
        out_specs=pl.BlockSpec((1, A_QW, tq), lambda b, qi, ki: (b, 0, qi)),
        out_shape=jax.ShapeDtypeStruct((bsz, A_QW, seq), BF16),
        scratch_shapes=[pltpu.VMEM((A_HEADS, A_KW, tq), BF16),
                        pltpu.VMEM((A_HEADS, 1, tq), F32),
                        pltpu.VMEM((A_HEADS, 1, tq), F32),
                        pltpu.VMEM((A_HEADS, HEAD_DIM, tq), F32)],
        compiler_params=_cparams(("parallel", "parallel", "arbitrary")),
        name="attn_a",
    )(qt, k, vt)


def _alibi_slope(h):
    return 2.0 ** (-8.0 * (h + 1) / B_HEADS)


def _attn_b_kernel(lam_ref, qt_ref, k_ref, vt_ref, gain_ref, ot_ref, qpad, m_ref, l_ref, acc_ref, *, out_scale):
    qi = pl.program_id(1)
    ki = pl.program_id(2)
    tk, tq = k_ref.shape[1], qt_ref.shape[2]

    @pl.when(ki == 0)
    def _():
        qpad[...] = jnp.zeros(qpad.shape, BF16)
        for mp in range(B_MAPS):
            qpad[mp, mp * B_QK_DIM:(mp + 1) * B_QK_DIM, :] = qt_ref[0, mp * B_QK_DIM:(mp + 1) * B_QK_DIM, :]
        _attn_init(m_ref, l_ref, acc_ref)

    k = k_ref[0]
    rel = (lax.broadcasted_iota(jnp.int32, (tk, tq), 1) - lax.broadcasted_iota(jnp.int32, (tk, tq), 0)
           + (qi * tq - ki * tk))
    dist = jnp.abs(rel).astype(F32)
    for h in range(B_HEADS):
        bias = dist * (-_alibi_slope(h))
        vt = vt_ref[0, h * B_V_DIM:(h + 1) * B_V_DIM, :]
        for c in range(2):
            mp = 2 * h + c
            s = _dot(k, qpad[mp]) + bias
            _online_softmax_step(s, vt, m_ref, l_ref, acc_ref, mp)

    @pl.when(ki == pl.num_programs(2) - 1)
    def _():
        lam = lam_ref[0, 0]
        for h in range(B_HEADS):
            o = acc_ref[2 * h] / l_ref[2 * h] - lam * (acc_ref[2 * h + 1] / l_ref[2 * h + 1])
            ms = jnp.mean(o * o, axis=0, keepdims=True)
            o = o * lax.rsqrt(ms + NORM_EPS) * gain_ref[...] * out_scale
            ot_ref[0, h * B_V_DIM:(h + 1) * B_V_DIM, :] = o.astype(BF16)


def _attn_b(lam, qt, k, vt, gain_col, out_scale):
    bsz, _, seq = qt.shape
    tq, tk = min(ATT_TQ, seq), min(ATT_TK, seq)
    return pl.pallas_call(
        functools.partial(_attn_b_kernel, out_scale=out_scale),
        grid=(bsz, seq // tq, seq // tk),
        in_specs=[pl.BlockSpec((1, 1), lambda b, qi, ki: (0, 0), memory_space=pltpu.SMEM),
                  pl.BlockSpec((1, B_QW, tq), lambda b, qi, ki: (b, 0, qi)),
                  pl.BlockSpec((1, tk, B_QW), lambda b, qi, ki: (b, ki, 0)),
                  pl.BlockSpec((1, B_VW, tk), lambda b, qi, ki: (b, 0, ki)),
                  pl.BlockSpec((B_V_DIM, 1), lambda b, qi, ki: (0, 0))],
        out_specs=pl.BlockSpec((1, B_VW, tq), lambda b, qi, ki: (b, 0, qi)),
        out_shape=jax.ShapeDtypeStruct((bsz, B_VW, seq), BF16),
        scratch_shapes=[pltpu.VMEM((B_MAPS, B_QW, tq), BF16),
                        pltpu.VMEM((B_MAPS, 1, tq), F32),
                        pltpu.VMEM((B_MAPS, 1, tq), F32),
                        pltpu.VMEM((B_MAPS, B_V_DIM, tq), F32)],
        compiler_params=_cparams(("parallel", "parallel", "arbitrary")),
        name="attn_b",
    )(lam, qt, k, vt, gain_col)


def _gla_masks(reverse):
    j = np.arange(C_CHUNK)
    tri = (j[None, :] >= j[:, None]) if reverse else (j[None, :] <= j[:, None])
    tri = tri.astype(np.float32)
    d = np.arange(C_KP)
    dhead = np.where(d < C_KW, d // C_K_DIM, -1)
    hj = np.arange(C_HEADS * C_CHUNK) // C_CHUNK
    vhead = np.arange(C_VW) // C_V_DIM
    m_stack = (hj[:, None] == dhead[None, :]).astype(np.float32)
    m_p = np.tile(tri, (1, C_HEADS))
    m_v = (hj[:, None] == vhead[None, :]).astype(np.float32)
    m_s = (vhead[:, None] == dhead[None, :]).astype(np.float32)
    return (jnp.asarray(tri, BF16), jnp.asarray(m_stack, F32), jnp.asarray(m_p, F32), jnp.asarray(m_v, F32),
            jnp.asarray(m_s, F32))


def _gla_kernel(q_ref, k_ref, v_ref, g_ref, tri_ref, mstack_ref, mp_ref, mv_ref, ms_ref, o_ref, st_ref, *,
                reverse):
    @pl.when(pl.program_id(1) == 0)
    def _():
        st_ref[...] = jnp.zeros(st_ref.shape, F32)

    n_chunks = q_ref.shape[1] // C_CHUNK
    edge = 0 if reverse else C_CHUNK - 1

    def chunk(ci, carry):
        c = (n_chunks - 1 - ci) if reverse else ci
        rows = pl.ds(pl.multiple_of(c * C_CHUNK, C_CHUNK), C_CHUNK)
        q = q_ref[0, rows, :]
        k = k_ref[0, rows, :]
        v = v_ref[0, rows, :]
        g = g_ref[0, rows, :]
        tri = tri_ref[...]
        g_hi, g_mid, g_lo = _split3(g)
        b = _dot(tri, g_hi) + _dot(tri, g_mid) + _dot(tri, g_lo)
        b_edge = b[edge:edge + 1, :]
        a = (q * jnp.exp(b)).astype(BF16)
        bm = k * jnp.exp(-b)
        khat = (k * jnp.exp(b_edge - b)).astype(BF16)
        bm_stack = (jnp.concatenate([bm] * C_HEADS, axis=0) * mstack_ref[...]).astype(BF16)
        p = (_dot_nt(a, bm_stack) * mp_ref[...]).astype(BF16)
        v_bd = (jnp.concatenate([v] * C_HEADS, axis=0) * mv_ref[...]).astype(BF16)
        st = st_ref[...]
        o_ref[0, rows, :] = _dot(p, v_bd) + _dot_nt(a, st.astype(BF16))
        st_ref[...] = st * jnp.exp(b_edge) + _dot_tn(v.astype(BF16), khat) * ms_ref[...]
        return carry

    lax.fori_loop(0, n_chunks, chunk, 0)


def _gla(q, k, v, g, reverse):
    bsz, seq, _ = q.shape
    blk = min(GLA_BLOCK, seq)
    nblk = seq // blk
    masks = _gla_masks(reverse)
    tmap = (lambda b, t: (b, nblk - 1 - t, 0)) if reverse else (lambda b, t: (b, t, 0))
    return pl.pallas_call(
        functools.partial(_gla_kernel, reverse=reverse),
        grid=(bsz, nblk),
        in_specs=[pl.BlockSpec((1, blk, C_KP), tmap), pl.BlockSpec((1, blk, C_KP), tmap),
                  pl.BlockSpec((1, blk, C_VW), tmap), pl.BlockSpec((1, blk, C_KP), tmap)]
                 + [_full(m.shape) for m in masks],
        out_specs=pl.BlockSpec((1, blk, C_VW), tmap),
        out_shape=jax.ShapeDtypeStruct((bsz, seq, C_VW), F32),
        scratch_shapes=[pltpu.VMEM((C_VW, C_KP), F32)],
        compiler_params=_cparams(("parallel", "arbitrary")),
        name="gla_bwd" if reverse else "gla_fwd",
    )(q, k, v, g, *masks)


def _layer_norm(x, g, b):
    xc = x - jnp.mean(x, axis=-1, keepdims=True)
    var = jnp.mean(xc * xc, axis=-1, keepdims=True)
    return xc * lax.rsqrt(var + NORM_EPS) * g + b


def _outproj_kernel(x_ref, oa_ref, ob_ref, ocf_ref, ocb_ref, og_ref, cg_ref, e96_ref, woa_ref, wob_ref, woc_ref,
                    g1_ref, b1_ref, wrh_ref, wrl_ref, br_ref, x1_ref, eid_ref, gate_ref, *, alpha):
    oc = ocf_ref[...] + ocb_ref[...]
    oc = oc * lax.rsqrt(_seg_mean(oc * oc, e96_ref[...]) + NORM_EPS) * cg_ref[...] * og_ref[...]
    mixed = (_dot(oa_ref[...], woa_ref[...]) + _dot(ob_ref[...], wob_ref[...])
             + _dot(oc.astype(BF16), woc_ref[...]))
    x1 = _layer_norm(alpha * x_ref[...] + mixed, g1_ref[...], b1_ref[...])
    x1_ref[...] = x1

    x_hi, x_lo = _split2(x1)
    wrh = wrh_ref[...]
    lt = _dot_nt(wrh, x_hi) + _dot_nt(wrh, x_lo) + _dot_nt(wrl_ref[...], x_hi) + br_ref[...]
    tm = lt.shape[1]
    coarse = lt[0:N_GROUPS]
    r4 = lax.broadcasted_iota(jnp.int32, (N_GROUPS, tm), 0)
    cmax = jnp.max(coarse, axis=0, keepdims=True)
    group = jnp.min(jnp.where(coarse == cmax, r4, N_GROUPS), axis=0, keepdims=True)
    group_w = 1.0 / jnp.sum(jnp.exp(coarse - cmax), axis=0, keepdims=True)
    fine = jnp.zeros((EXPERTS_PER_GROUP, tm), F32)
    for g in range(N_GROUPS):
        fine = jnp.where(group == g, lt[8 + 8 * g:16 + 8 * g], fine)
    r8 = lax.broadcasted_iota(jnp.int32, (EXPERTS_PER_GROUP, tm), 0)
    v1 = jnp.max(fine, axis=0, keepdims=True)
    i1 = jnp.min(jnp.where(fine == v1, r8, EXPERTS_PER_GROUP), axis=0, keepdims=True)
    rest = jnp.where(r8 == i1, -jnp.inf, fine)
    v2 = jnp.max(rest, axis=0, keepdims=True)
    i2 = jnp.min(jnp.where(rest == v2, r8, EXPERTS_PER_GROUP), axis=0, keepdims=True)
    e2 = jnp.exp(v2 - v1)
    w1 = group_w / (1.0 + e2)
    w2 = group_w * e2 / (1.0 + e2)
    eid_ref[...] = jnp.where(r8 == 0, group * EXPERTS_PER_GROUP + i1,
                             jnp.where(r8 == 1, group * EXPERTS_PER_GROUP + i2, 0))
    gate_ref[...] = jnp.where(r8 == 0, w1, jnp.where(r8 == 1, w2, 0.0))


def _outproj(x, oa, ob, ocf, ocb, og, cg, e96, woa, wob, woc, g1, b1, wrh, wrl, br, alpha):
    n = x.shape[0]
    tm = ROW_TILE
    row = lambda w: pl.BlockSpec((tm, w), lambda i: (i, 0))
    col = pl.BlockSpec((8, tm), lambda i: (0, i))
    return pl.pallas_call(
        functools.partial(_outproj_kernel, alpha=alpha),
        grid=(n // tm,),
        in_specs=[row(D_MODEL), row(A_QW), row(B_VW), row(C_VW), row(C_VW), row(C_VW), _full(cg.shape),
                  _full(e96.shape), _full(woa.shape), _full(wob.shape), _full(woc.shape), _full(g1.shape),
                  _full(b1.shape), _full(wrh.shape), _full(wrl.shape), _full(br.shape)],
        out_specs=[row(D_MODEL), col, col],
        out_shape=[jax.ShapeDtypeStruct((n, D_MODEL), F32), jax.ShapeDtypeStruct((8, n), jnp.int32),
                   jax.ShapeDtypeStruct((8, n), F32)],
        compiler_params=_cparams(("parallel",)),
        name="outproj",
    )(x, oa, ob, ocf, ocb, og, cg, e96, woa, wob, woc, g1, b1, wrh, wrl, br)


def _moe_kernel(be_ref, tok_ref, tokn_ref, dst_ref, gate_ref, x_hbm, wg_ref, wu_ref, wd_ref, y_hbm,
                xbuf, ybuf, gsem, ssem):
    i = pl.program_id(0)
    nb = pl.num_programs(0)
    bm = xbuf.shape[1]
    slot = i % 2

    def gather(idx_ref, sl, start):
        def body(r, carry):
            cp = pltpu.make_async_copy(x_hbm.at[pl.ds(idx_ref[0, 0, r], 1), :], xbuf.at[sl, pl.ds(r, 1), :],
                                       gsem.at[sl])
            if start:
                cp.start()
            else:
                cp.wait()
            return carry
        lax.fori_loop(0, bm, body, 0)

    def scatter(sl, start):
        def body(r, carry):
            cp = pltpu.make_async_copy(ybuf.at[sl, pl.ds(r, 1), :], y_hbm.at[pl.ds(dst_ref[0, 0, r], 1), :],
                                       ssem.at[sl])
            if start:
                cp.start()
            else:
                cp.wait()
            return carry
        lax.fori_loop(0, bm, body, 0)

    @pl.when(i == 0)
    def _():
        gather(tok_ref, 0, True)

    @pl.when(i + 1 < nb)
    def _():
        gather(tokn_ref, 1 - slot, True)

    gather(tok_ref, slot, False)

    @pl.when(i >= 2)
    def _():
        scatter(slot, False)

    xb = xbuf[slot].astype(BF16)
    hg = _dot(xb, wg_ref[0])
    hu = _dot(xb, wu_ref[0])
    h = (hg * (1.0 / (1.0 + jnp.exp(-hg))) * hu).astype(BF16)
    ybuf[slot] = _dot(h, wd_ref[0]) * gate_ref[...]
    scatter(slot, True)

    @pl.when(i == nb - 1)
    def _():
        scatter(slot, False)

        @pl.when(nb >= 2)
        def _():
            scatter(1 - slot, False)


def _moe(block_expert, slot_tok, slot_dst, slot_gate, x1, wg, wu, wd, n_rows_out):
    nb = slot_tok.shape[0]
    bm = MOE_BM
    grid_spec = pltpu.PrefetchScalarGridSpec(
        num_scalar_prefetch=1,
        grid=(nb,),
        in_specs=[pl.BlockSpec((1, 1, bm), lambda i, be: (i, 0, 0), memory_space=pltpu.SMEM),
                  pl.BlockSpec((1, 1, bm), lambda i, be: (jnp.minimum(i + 1, nb - 1), 0, 0),
                               memory_space=pltpu.SMEM),
                  pl.BlockSpec((1, 1, bm), lambda i, be: (i, 0, 0), memory_space=pltpu.SMEM),
                  pl.BlockSpec((bm, 1), lambda i, be: (i, 0)),
                  pl.BlockSpec(memory_space=pl.ANY),
                  pl.BlockSpec((1, D_MODEL, D_EXPERT), lambda i, be: (be[i], 0, 0)),
                  pl.BlockSpec((1, D_MODEL, D_EXPERT), lambda i, be: (be[i], 0, 0)),
                  pl.BlockSpec((1, D_EXPERT, D_MODEL), lambda i, be: (be[i], 0, 0))],
        out_specs=pl.BlockSpec(memory_space=pl.ANY),
        scratch_shapes=[pltpu.VMEM((2, bm, D_MODEL), F32), pltpu.VMEM((2, bm, D_MODEL), F32),
                        pltpu.SemaphoreType.DMA((2,)), pltpu.SemaphoreType.DMA((2,))],
    )
    return pl.pallas_call(
        _moe_kernel,
        grid_spec=grid_spec,
        out_shape=jax.ShapeDtypeStruct((n_rows_out, D_MODEL), F32),
        compiler_params=_cparams(("arbitrary",)),
        name="moe_ffn",
    )(block_expert, slot_tok, slot_tok, slot_dst, slot_gate, x1, wg, wu, wd)


def _route_slots(eid, gate, n_tok):
    bm = MOE_BM
    n_assign = 2 * n_tok
    flat_e = eid[0:2].reshape(-1)
    flat_g = gate[0:2].reshape(-1)
    order = jnp.argsort(flat_e).astype(jnp.int32)
    sorted_e = flat_e[order]
    counts = jnp.bincount(flat_e, length=N_EXPERTS).astype(jnp.int32)
    seg_start = jnp.cumsum(counts) - counts
    padded = (counts + bm - 1) // bm * bm
    pad_end = jnp.cumsum(padded)
    pad_start = pad_end - padded
    dest = pad_start[sorted_e] + jnp.arange(n_assign, dtype=jnp.int32) - seg_start[sorted_e]
    nb = n_assign // bm + N_EXPERTS
    n_slots = nb * bm
    is_pad = jnp.ones((n_slots,), jnp.int32).at[dest].set(0)
    pad_dst = n_assign + jnp.cumsum(is_pad) - 1
    slot_tok = jnp.zeros((n_slots,), jnp.int32).at[dest].set(order % n_tok)
    slot_dst = pad_dst.astype(jnp.int32).at[dest].set(order)
    slot_gate = jnp.zeros((n_slots,), F32).at[dest].set(flat_g[order])
    block_expert = jnp.minimum(
        jnp.searchsorted(pad_end, jnp.arange(nb, dtype=jnp.int32) * bm, side='right'), N_EXPERTS - 1
    ).astype(jnp.int32)
    return (block_expert, slot_tok.reshape(nb, 1, bm), slot_dst.reshape(nb, 1, bm), slot_gate.reshape(n_slots, 1))


def _final_kernel(x1_ref, y0_ref, y1_ref, p_ref, wpg_ref, wpp_ref, g2_ref, b2_ref, o_ref, *, alpha):
    x1 = x1_ref[...]
    gate = 1.0 / (1.0 + jnp.exp(-_dot(x1.astype(BF16), wpg_ref[...])))
    ple = gate * _dot(p_ref[...].astype(BF16), wpp_ref[...])
    o_ref[...] = _layer_norm(alpha * x1 + (y0_ref[...] + y1_ref[...]) + ple, g2_ref[...], b2_ref[...])


def _final(x1, y2, p, wpg, wpp, g2, b2, alpha):
    n = x1.shape[0]
    tm = ROW_TILE
    nt = n // tm
    row = lambda w: pl.BlockSpec((tm, w), lambda i: (i, 0))
    return pl.pallas_call(
        functools.partial(_final_kernel, alpha=alpha),
        grid=(nt,),
        in_specs=[row(D_MODEL), row(D_MODEL), pl.BlockSpec((tm, D_MODEL), lambda i: (i + nt, 0)), row(PLE_DIM),
                  _full(wpg.shape), _full(wpp.shape), _full(g2.shape), _full(b2.shape)],
        out_specs=row(D_MODEL),
        out_shape=jax.ShapeDtypeStruct((n, D_MODEL), F32),
        compiler_params=_cparams(("parallel",)),
        name="final",
    )(x1, y2, y2, p, wpg, wpp, g2, b2)


def _rope_tables(seq_len):
    rows = seq_len // GRID_W
    row = jnp.repeat(jnp.arange(rows, dtype=F32), GRID_W)
    col = jnp.tile(jnp.arange(GRID_W, dtype=F32), rows)
    n_pairs = HEAD_DIM // 4
    inv_freq = ROPE_THETA ** (-jnp.arange(n_pairs, dtype=F32) / n_pairs)
    ang = jnp.concatenate([row[:, None] * inv_freq, col[:, None] * inv_freq], axis=-1)
    cos = jnp.repeat(jnp.cos(ang), 2, axis=-1)
    sin = jnp.repeat(jnp.sin(ang), 2, axis=-1) * jnp.tile(jnp.asarray([-1.0, 1.0], F32), HEAD_DIM // 2)
    return jnp.tile(cos, (1, 2)), jnp.tile(sin, (1, 2))


def _pad_cols(w, width):
    return jnp.pad(w, ((0, 0), (0, width - w.shape[1])))


def kernel(x_prompt, x_sample, p_prompt, p_sample, w_in, a_q_norm, a_k_norm, b_lambda, b_subln, c_gate_w2,
           c_gate_b, c_norm, w_out, ln1_g, ln1_b, w_router_coarse, b_router_coarse, w_router_fine, b_router_fine,
           w_exp_gate, w_exp_up, w_exp_down, w_ple_gate, w_ple_proj, ln2_g, ln2_b):
    depth = w_in.shape[0]
    alpha = (2.0 * depth) ** 0.25
    bp, tp, _ = x_prompt.shape
    bs, ts, _ = x_sample.shape
    n_p, n_s = bp * tp, bs * ts
    n_tok = n_p + n_s
    tm = ROW_TILE
    assert tp % tm == 0 and ts % tm == 0 and tp % GRID_W == 0 and ts % GRID_W == 0
    assert (2 * n_tok) % MOE_BM == 0

    cos_t, sin_t = _rope_tables(max(tp, ts))
    np_tiles, tp_tiles, ts_tiles = n_p // tm, tp // tm, ts // tm

    def pos_block(i):
        return jnp.where(i < np_tiles, i % tp_tiles, (i - np_tiles) % ts_tiles)

    e64 = _block_diag_avg(A_QW, HEAD_DIM)
    e96 = _block_diag_avg(C_VW, C_V_DIM)
    x = jnp.concatenate([x_prompt.reshape(n_p, D_MODEL), x_sample.reshape(n_s, D_MODEL)], axis=0)

    def per_group(t, width):
        return t[:n_p].reshape(bp, tp, width), t[n_p:].reshape(bs, ts, width)

    def transposed(t, width):
        a, b = per_group(t, width)
        return a.transpose(0, 2, 1), b.transpose(0, 2, 1)

    def merge_t(a, b, width):
        return jnp.concatenate([a.transpose(0, 2, 1).reshape(n_p, width),
                                b.transpose(0, 2, 1).reshape(n_s, width)], axis=0)

    def merge(a, b, width):
        return jnp.concatenate([a.reshape(n_p, width), b.reshape(n_s, width)], axis=0)

    for i in range(depth):
        offs = np.cumsum([0, A_QW, A_KW, A_KW, B_QW, B_QW, B_VW, C_KW, C_KW, C_VW, C_GATE_RANK, C_GATE_RANK, C_VW])
        cols = [w_in[i][:, offs[j]:offs[j + 1]] for j in range(12)]
        wa = jnp.concatenate(cols[0:3], axis=1).astype(BF16)
        wb = jnp.concatenate(cols[3:6], axis=1).astype(BF16)
        wc = jnp.concatenate([_pad_cols(cols[6], C_KP), _pad_cols(cols[7], C_KP), cols[8], cols[11],
                              _pad_cols(jnp.concatenate([cols[9], cols[10]], axis=1), 128)], axis=1).astype(BF16)
        w2 = c_gate_w2[i]
        w2f = jnp.zeros((128, C_KP), F32).at[:C_GATE_RANK, :C_KW].set(w2[0]).astype(BF16)
        w2b = jnp.zeros((128, C_KP), F32).at[C_GATE_RANK:2 * C_GATE_RANK, :C_KW].set(w2[1]).astype(BF16)
        gbf = _pad_cols(c_gate_b[i, 0][None, :], C_KP)
        gbb = _pad_cols(c_gate_b[i, 1][None, :], C_KP)
        gq = jnp.tile(a_q_norm[i], A_HEADS)[None, :]
        gk = jnp.tile(a_k_norm[i], A_KV_HEADS)[None, :]

        (qa, ka, va, qb, kb, vb, cq, ck, cv, gf, gb, og) = _inproj(
            x, wa, wb, wc, w2f, w2b, gbf, gbb, gq, gk, e64, cos_t, sin_t, pos_block)

        qa_p, qa_s = transposed(qa, A_QW)
        ka_p, ka_s = per_group(ka, A_KW)
        va_p, va_s = transposed(va, A_KW)
        oa = merge_t(_attn_a(qa_p, ka_p, va_p), _attn_a(qa_s, ka_s, va_s), A_QW)

        lam_init = 0.8 - 0.6 * math.exp(-0.3 * i)
        lv = b_lambda[i].astype(F32)
        lam = (jnp.exp(jnp.sum(lv[0] * lv[1])) - jnp.exp(jnp.sum(lv[2] * lv[3])) + lam_init).reshape(1, 1)
        gain_col = b_subln[i].reshape(B_V_DIM, 1)
        qb_p, qb_s = transposed(qb, B_QW)
        kb_p, kb_s = per_group(kb, B_QW)
        vb_p, vb_s = transposed(vb, B_VW)
        ob = merge_t(_attn_b(lam, qb_p, kb_p, vb_p, gain_col, 1.0 - lam_init),
                     _attn_b(lam, qb_s, kb_s, vb_s, gain_col, 1.0 - lam_init), B_VW)

        cq_p, cq_s = per_group(cq, C_KP)
        ck_p, ck_s = per_group(ck, C_KP)
        cv_p, cv_s = per_group(cv, C_VW)
        gf_p, gf_s = per_group(gf, C_KP)
        gb_p, gb_s = per_group(gb, C_KP)
        ocf = merge(_gla(cq_p, ck_p, cv_p, gf_p, False), _gla(cq_s, ck_s, cv_s, gf_s, False), C_VW)
        ocb = merge(_gla(cq_p, ck_p, cv_p, gb_p, True), _gla(cq_s, ck_s, cv_s, gb_s, True), C_VW)

        wo = w_out[i].astype(BF16)
        wr = jnp.zeros((ROUTE_ROWS, D_MODEL), F32)
        wr = wr.at[0:N_GROUPS].set(w_router_coarse[i].T).at[8:].set(w_router_fine[i].T)
        wrh = wr.astype(BF16)
        wrl = (wr - wrh.astype(F32)).astype(BF16)
        br = jnp.zeros((ROUTE_ROWS, 1), F32)
        br = br.at[0:N_GROUPS, 0].set(b_router_coarse[i].astype(F32)).at[8:, 0].set(b_router_fine[i].astype(F32))
        x1, eid, gate = _outproj(
            x, oa, ob, ocf, ocb, og, jnp.tile(c_norm[i], C_HEADS)[None, :], e96,
            wo[:A_QW], wo[A_QW:A_QW + B_VW], wo[A_QW + B_VW:], ln1_g[i][None, :], ln1_b[i][None, :],
            wrh, wrl, br, alpha)

        block_expert, slot_tok, slot_dst, slot_gate = _route_slots(eid, gate, n_tok)
        y2 = _moe(block_expert, slot_tok, slot_dst, slot_gate, x1, w_exp_gate[i].astype(BF16),
                  w_exp_up[i].astype(BF16), w_exp_down[i].astype(BF16), slot_gate.shape[0])

        p_all = jnp.concatenate([p_prompt[i].reshape(n_p, PLE_DIM), p_sample[i].reshape(n_s, PLE_DIM)], axis=0)
        x = _final(x1, y2, p_all, w_ple_gate[i].astype(BF16), w_ple_proj[i].astype(BF16),
                   ln2_g[i][None, :], ln2_b[i][None, :], alpha)

    return x[:n_p].reshape(bp, tp, D_MODEL), x[n_p:].reshape(bs, ts, D_MODEL)
```

```python
import functools
import math

import numpy as np
import jax
import jax.numpy as jnp
from jax import lax
from jax.experimental import pallas as pl
from jax.experimental.pallas import tpu as pltpu

F32 = jnp.float32
BF16 = jnp.bfloat16

D_MODEL = 1024
GRID_W = 64
HEAD_DIM = 64
NORM_EPS = 1e-6
A_HEADS = 6
A_KV_HEADS = 2
A_GROUP = A_HEADS // A_KV_HEADS
ROPE_THETA = 10000.0
B_HEADS = 4
B_QK_DIM = 32
B_V_DIM = 64
B_MAPS = 2 * B_HEADS
C_HEADS = 4
C_K_DIM = 48
C_V_DIM = 96
C_GATE_RANK = 16
C_GATE_TAU = 16.0
C_CHUNK = 64
C_KW = C_HEADS * C_K_DIM
C_KP = 256
C_VW = C_HEADS * C_V_DIM
N_GROUPS = 4
EXPERTS_PER_GROUP = 8
N_EXPERTS = N_GROUPS * EXPERTS_PER_GROUP
D_EXPERT = 512
PLE_DIM = 256
A_QW = A_HEADS * HEAD_DIM
A_KW = A_KV_HEADS * HEAD_DIM
B_QW = B_HEADS * 2 * B_QK_DIM
B_VW = B_HEADS * B_V_DIM
ROUTE_ROWS = 8 + N_EXPERTS

VMEM_LIMIT = 56 * 1024 * 1024

ROW_TILE = 512
ATT_TQ = 512
ATT_COL = 256
ATT_KS_A = 512
ATT_KS_B = 256
ATT_VROWS = HEAD_DIM + 16
LOG2E = math.log2(math.e)
GLA_BLOCK = 512
MOE_BM = 256


def _cparams(sem):
    return pltpu.CompilerParams(dimension_semantics=sem, vmem_limit_bytes=VMEM_LIMIT)


def _full(shape):
    nd = len(shape)
    return pl.BlockSpec(shape, lambda *_: (0,) * nd)


def _dot(a, b):
    return jnp.dot(a, b, preferred_element_type=F32)


def _dot_nt(a, b):
    return lax.dot_general(a, b, (((1,), (1,)), ((), ())), preferred_element_type=F32)


def _dot_tn(a, b):
    return lax.dot_general(a, b, (((0,), (0,)), ((), ())), preferred_element_type=F32)


def _split2(x):
    hi = x.astype(BF16)
    lo = (x - hi.astype(F32)).astype(BF16)
    return hi, lo


def _split3(x):
    hi = x.astype(BF16)
    r = x - hi.astype(F32)
    mid = r.astype(BF16)
    lo = (r - mid.astype(F32)).astype(BF16)
    return hi, mid, lo


def _seg_mean(sq, e):
    hi, lo = _split2(sq)
    return _dot(hi, e) + _dot(lo, e)


def _block_diag_avg(width, seg):
    idx = np.arange(width) // seg
    return jnp.asarray((idx[:, None] == idx[None, :]).astype(np.float32) / seg, dtype=BF16)


def _rope(x, cos, sin_signed):
    pieces = []
    for c in range(x.shape[1] // 128):
        xc = x[:, c * 128:(c + 1) * 128]
        nxt = pltpu.roll(xc, 127, 1)
        prv = pltpu.roll(xc, 1, 1)
        lane = lax.broadcasted_iota(jnp.int32, xc.shape, 1)
        sw = jnp.where(lane % 2 == 0, nxt, prv)
        pieces.append(xc * cos + sw * sin_signed)
    return pieces[0] if len(pieces) == 1 else jnp.concatenate(pieces, axis=1)


def _inproj_kernel(x_ref, wa_ref, wb_ref, wc_ref, w2f_ref, w2b_ref, gbf_ref, gbb_ref, gq_ref, gk_ref,
                   e64_ref, cos_ref, sin_ref,
                   qa_ref, ka_ref, va_ref, qb_ref, kb_ref, vb_ref, cq_ref, ck_ref, cv_ref, gf_ref, gb_ref,
                   og_ref):
    xb = x_ref[...].astype(BF16)
    cos = cos_ref[...]
    sin = sin_ref[...]

    za = _dot(xb, wa_ref[...])
    q = za[:, :A_QW]
    k = za[:, A_QW:A_QW + A_KW]
    e64 = e64_ref[...]
    qn = q * lax.rsqrt(_seg_mean(q * q, e64) + NORM_EPS) * gq_ref[...]
    kn = k * lax.rsqrt(_seg_mean(k * k, e64[:A_KW, :A_KW]) + NORM_EPS) * gk_ref[...]
    qa_ref[...] = (_rope(qn, cos, sin) * (HEAD_DIM ** -0.5 * LOG2E)).astype(BF16)
    ka_ref[...] = _rope(kn, cos, sin).astype(BF16)
    va_ref[...] = za[:, A_QW + A_KW:].astype(BF16)

    zb = _dot(xb, wb_ref[...])
    qb_ref[...] = (zb[:, :B_QW] * (B_QK_DIM ** -0.5 * LOG2E)).astype(BF16)
    kb_ref[...] = zb[:, B_QW:2 * B_QW].astype(BF16)
    vb_ref[...] = zb[:, 2 * B_QW:].astype(BF16)

    zc = _dot(xb, wc_ref[...])
    cq_ref[...] = zc[:, :C_KP] * (C_K_DIM ** -0.5)
    ck_ref[...] = zc[:, C_KP:2 * C_KP]
    cv_ref[...] = zc[:, 2 * C_KP:2 * C_KP + C_VW]
    og = zc[:, 2 * C_KP + C_VW:2 * C_KP + 2 * C_VW]
    og_ref[...] = og * (1.0 / (1.0 + jnp.exp(-og)))
    lr = zc[:, 2 * C_KP + 2 * C_VW:].astype(BF16)

    def log_decay(w2_ref, bias_ref):
        g = _dot(lr, w2_ref[...]) + bias_ref[...]
        return (jnp.minimum(g, 0.0) - jnp.log(1.0 + jnp.exp(-jnp.abs(g)))) * (1.0 / C_GATE_TAU)

    gf_ref[...] = log_decay(w2f_ref, gbf_ref)
    gb_ref[...] = log_decay(w2b_ref, gbb_ref)


def _inproj(x, wa, wb, wc, w2f, w2b, gbf, gbb, gq, gk, e64, cos_t, sin_t, pos_block):
    n = x.shape[0]
    tm = ROW_TILE
    row = lambda w: pl.BlockSpec((tm, w), lambda i: (i, 0))
    tab = pl.BlockSpec((tm, 128), lambda i: (pos_block(i), 0))
    outs = [(A_QW, BF16), (A_KW, BF16), (A_KW, BF16), (B_QW, BF16), (B_QW, BF16), (B_VW, BF16),
            (C_KP, F32), (C_KP, F32), (C_VW, F32), (C_KP, F32), (C_KP, F32), (C_VW, F32)]
    return pl.pallas_call(
        _inproj_kernel,
        grid=(n // tm,),
        in_specs=[row(D_MODEL), _full(wa.shape), _full(wb.shape), _full(wc.shape), _full(w2f.shape),
                  _full(w2b.shape), _full(gbf.shape), _full(gbb.shape), _full(gq.shape), _full(gk.shape),
                  _full(e64.shape), tab, tab],
        out_specs=[row(w) for w, _ in outs],
        out_shape=[jax.ShapeDtypeStruct((n, w), dt) for w, dt in outs],
        compiler_params=_cparams(("parallel",)),
        name="inproj",
    )(x, wa, wb, wc, w2f, w2b, gbf, gbb, gq, gk, e64, cos_t, sin_t)


def _attn_loop(k_ref, vt_ref, qpad, s_scr, m_ref, acc_ref, ks, vhead_of_map, bias_fn=None):
    n_maps, _, tq = qpad.shape
    ncol = tq // ATT_COL
    n_sub = k_ref.shape[1] // ks
    m_ref[...] = jnp.full(m_ref.shape, -jnp.inf, F32)
    acc_ref[...] = jnp.zeros(acc_ref.shape, F32)

    k0 = k_ref[0, 0:ks, :]
    for mp in range(n_maps):
        for c in range(ncol):
            s_scr[mp, c] = _dot(k0, qpad[mp, :, c * ATT_COL:(c + 1) * ATT_COL])

    def key_tile(j, carry):
        jn = jnp.minimum(j + 1, n_sub - 1)
        kn = k_ref[0, pl.ds(pl.multiple_of(jn * ks, ks), ks), :]
        for c in range(ncol):
            cs = slice(c * ATT_COL, (c + 1) * ATT_COL)
            bias = None
            for mp in range(n_maps):
                vh = vhead_of_map[mp]
                if bias_fn is not None and (mp == 0 or vhead_of_map[mp - 1] != vh):
                    bias = bias_fn(j, c, vh)
                s = s_scr[mp, c]
                s_scr[mp, c] = _dot(kn, qpad[mp, :, cs])
                if bias is not None:
                    s = s + bias
                m_prev = m_ref[mp, :, cs]
                m_new = jnp.maximum(m_prev, jnp.max(s, axis=0, keepdims=True))
                p = jnp.exp2(s - m_new).astype(BF16)
                vt = vt_ref[0, j, vh * ATT_VROWS:(vh + 1) * ATT_VROWS, :]
                acc_ref[mp, :, cs] = jnp.exp2(m_prev - m_new) * acc_ref[mp, :, cs] + _dot(vt, p)
                m_ref[mp, :, cs] = m_new
        return carry

    lax.fori_loop(0, n_sub, key_tile, 0)


def _attn_out(acc_ref, mp):
    return acc_ref[mp, 0:HEAD_DIM, :] / acc_ref[mp, HEAD_DIM:HEAD_DIM + 1, :]


def _attn_a_kernel(qt_ref, k_ref, vt_ref, ot_ref, qpad, s_scr, m_ref, acc_ref, *, ks):
    qpad[...] = jnp.zeros(qpad.shape, BF16)
    for h in range(A_HEADS):
        g = h // A_GROUP
        qpad[h, g * HEAD_DIM:(g + 1) * HEAD_DIM, :] = qt_ref[0, h * HEAD_DIM:(h + 1) * HEAD_DIM, :]
    _attn_loop(k_ref, vt_ref, qpad, s_scr, m_ref, acc_ref, ks, tuple(h // A_GROUP for h in range(A_HEADS)))
    for h in range(A_HEADS):
        ot_ref[0, h * HEAD_DIM:(h + 1) * HEAD_DIM, :] = _attn_out(acc_ref, h).astype(BF16)


def _attn_scratch(n_maps, qk_width, tq, ks):
    return [pltpu.VMEM((n_maps, qk_width, tq), BF16),
            pltpu.VMEM((n_maps, tq // ATT_COL, ks, ATT_COL), F32),
            pltpu.VMEM((n_maps, 1, tq), F32),
            pltpu.VMEM((n_maps, ATT_VROWS, tq), F32)]


def _attn_a(qt, k, vt, ks):
    bsz, _, seq = qt.shape
    tq = min(ATT_TQ, seq)
    return pl.pallas_call(
        functools.partial(_attn_a_kernel, ks=ks),
        grid=(bsz, seq // tq),
        in_specs=[pl.BlockSpec((1, A_QW, tq), lambda b, qi: (b, 0, qi)),
                  pl.BlockSpec((1, seq, A_KW), lambda b, qi: (b, 0, 0)),
                  pl.BlockSpec((1, seq // ks, A_KV_HEADS * ATT_VROWS, ks), lambda b, qi: (b, 0, 0, 0))],
        out_specs=pl.BlockSpec((1, A_QW, tq), lambda b, qi: (b, 0, qi)),
        out_shape=jax.ShapeDtypeStruct((bsz, A_QW, seq), BF16),
        scratch_shapes=_attn_scratch(A_HEADS, A_KW, tq, ks),
        compiler_params=_cparams(("parallel", "arbitrary")),
        name="attn_a",
    )(qt, k, vt)


def _alibi_slope(h):
    return 2.0 ** (-8.0 * (h + 1) / B_HEADS)


def _attn_b_kernel(lam_ref, qt_ref, k_ref, vt_ref, gain_ref, ot_ref, qpad, s_scr, m_ref, acc_ref, *, ks,
                   out_scale):
    qi = pl.program_id(1)
    tq = qt_ref.shape[2]

    qpad[...] = jnp.zeros(qpad.shape, BF16)
    for mp in range(B_MAPS):
        qpad[mp, mp * B_QK_DIM:(mp + 1) * B_QK_DIM, :] = qt_ref[0, mp * B_QK_DIM:(mp + 1) * B_QK_DIM, :]

    def alibi(j, c, h):
        rel = (lax.broadcasted_iota(jnp.int32, (ks, ATT_COL), 1) - lax.broadcasted_iota(jnp.int32, (ks, ATT_COL), 0)
               + (qi * tq + c * ATT_COL - j * ks))
        return jnp.abs(rel).astype(F32) * (-_alibi_slope(h) * LOG2E)

    _attn_loop(k_ref, vt_ref, qpad, s_scr, m_ref, acc_ref, ks, tuple(mp // 2 for mp in range(B_MAPS)), alibi)

    lam = lam_ref[0, 0]
    for h in range(B_HEADS):
        o = _attn_out(acc_ref, 2 * h) - lam * _attn_out(acc_ref, 2 * h + 1)
        ms = jnp.mean(o * o, axis=0, keepdims=True)
        o = o * lax.rsqrt(ms + NORM_EPS) * gain_ref[...] * out_scale
        ot_ref[0, h * B_V_DIM:(h + 1) * B_V_DIM, :] = o.astype(BF16)


def _attn_b(lam, qt, k, vt, gain_col, out_scale, ks):
    bsz, _, seq = qt.shape
    tq = min(ATT_TQ, seq)
    return pl.pallas_call(
        functools.partial(_attn_b_kernel, ks=ks, out_scale=out_scale),
        grid=(bsz, seq // tq),
        in_specs=[pl.BlockSpec((1, 1), lambda b, qi: (0, 0), memory_space=pltpu.SMEM),
                  pl.BlockSpec((1, B_QW, tq), lambda b, qi: (b, 0, qi)),
                  pl.BlockSpec((1, seq, B_QW), lambda b, qi: (b, 0, 0)),
                  pl.BlockSpec((1, seq // ks, B_HEADS * ATT_VROWS, ks), lambda b, qi: (b, 0, 0, 0)),
                  pl.BlockSpec((B_V_DIM, 1), lambda b, qi: (0, 0))],
        out_specs=pl.BlockSpec((1, B_VW, tq), lambda b, qi: (b, 0, qi)),
        out_shape=jax.ShapeDtypeStruct((bsz, B_VW, seq), BF16),
        scratch_shapes=_attn_scratch(B_MAPS, B_QW, tq, ks),
        compiler_params=_cparams(("parallel", "arbitrary")),
        name="attn_b",
    )(lam, qt, k, vt, gain_col)


def _gla_masks(reverse):
    j = np.arange(C_CHUNK)
    tri = (j[None, :] >= j[:, None]) if reverse else (j[None, :] <= j[:, None])
    tri = tri.astype(np.float32)
    d = np.arange(C_KP)
    dhead = np.where(d < C_KW, d // C_K_DIM, -1)
    hj = np.arange(C_HEADS * C_CHUNK) // C_CHUNK
    vhead = np.arange(C_VW) // C_V_DIM
    m_stack = (hj[:, None] == dhead[None, :]).astype(np.float32)
    m_p = np.tile(tri, (1, C_HEADS))
    m_v = (hj[:, None] == vhead[None, :]).astype(np.float32)
    m_s = (vhead[:, None] == dhead[None, :]).astype(np.float32)
    return (jnp.asarray(tri, BF16), jnp.asarray(m_stack, F32), jnp.asarray(m_p, F32), jnp.asarray(m_v, F32),
            jnp.asarray(m_s, F32))


def _gla_kernel(q_ref, k_ref, v_ref, g_ref, tri_ref, mstack_ref, mp_ref, mv_ref, ms_ref, o_ref, st_ref, *,
                reverse):
    @pl.when(pl.program_id(1) == 0)
    def _():
        st_ref[...] = jnp.zeros(st_ref.shape, F32)

    n_chunks = q_ref.shape[1] // C_CHUNK
    edge = 0 if reverse else C_CHUNK - 1

    def chunk(ci, carry):
        c = (n_chunks - 1 - ci) if reverse else ci
        rows = pl.ds(pl.multiple_of(c * C_CHUNK, C_CHUNK), C_CHUNK)
        q = q_ref[0, rows, :]
        k = k_ref[0, rows, :]
        v = v_ref[0, rows, :]
        g = g_ref[0, rows, :]
        tri = tri_ref[...]
        g_hi, g_mid, g_lo = _split3(g)
        b = _dot(tri, g_hi) + _dot(tri, g_mid) + _dot(tri, g_lo)
        b_edge = b[edge:edge + 1, :]
        a = (q * jnp.exp(b)).astype(BF16)
        bm = k * jnp.exp(-b)
        khat = (k * jnp.exp(b_edge - b)).astype(BF16)
        bm_stack = (jnp.concatenate([bm] * C_HEADS, axis=0) * mstack_ref[...]).astype(BF16)
        p = (_dot_nt(a, bm_stack) * mp_ref[...]).astype(BF16)
        v_bd = (jnp.concatenate([v] * C_HEADS, axis=0) * mv_ref[...]).astype(BF16)
        st = st_ref[...]
        o_ref[0, rows, :] = _dot(p, v_bd) + _dot_nt(a, st.astype(BF16))
        st_ref[...] = st * jnp.exp(b_edge) + _dot_tn(v.astype(BF16), khat) * ms_ref[...]
        return carry

    lax.fori_loop(0, n_chunks, chunk, 0)


def _gla(q, k, v, g, reverse):
    bsz, seq, _ = q.shape
    blk = min(GLA_BLOCK, seq)
    nblk = seq // blk
    masks = _gla_masks(reverse)
    tmap = (lambda b, t: (b, nblk - 1 - t, 0)) if reverse else (lambda b, t: (b, t, 0))
    return pl.pallas_call(
        functools.partial(_gla_kernel, reverse=reverse),
        grid=(bsz, nblk),
        in_specs=[pl.BlockSpec((1, blk, C_KP), tmap), pl.BlockSpec((1, blk, C_KP), tmap),
                  pl.BlockSpec((1, blk, C_VW), tmap), pl.BlockSpec((1, blk, C_KP), tmap)]
                 + [_full(m.shape) for m in masks],
        out_specs=pl.BlockSpec((1, blk, C_VW), tmap),
        out_shape=jax.ShapeDtypeStruct((bsz, seq, C_VW), F32),
        scratch_shapes=[pltpu.VMEM((C_VW, C_KP), F32)],
        compiler_params=_cparams(("parallel", "arbitrary")),
        name="gla_bwd" if reverse else "gla_fwd",
    )(q, k, v, g, *masks)


def _layer_norm(x, g, b):
    xc = x - jnp.mean(x, axis=-1, keepdims=True)
    var = jnp.mean(xc * xc, axis=-1, keepdims=True)
    return xc * lax.rsqrt(var + NORM_EPS) * g + b


def _outproj_kernel(x_ref, oa_ref, ob_ref, ocf_ref, ocb_ref, og_ref, cg_ref, e96_ref, woa_ref, wob_ref, woc_ref,
                    g1_ref, b1_ref, wrh_ref, wrl_ref, br_ref, x1_ref, eid_ref, gate_ref, *, alpha):
    oc = ocf_ref[...] + ocb_ref[...]
    oc = oc * lax.rsqrt(_seg_mean(oc * oc, e96_ref[...]) + NORM_EPS) * cg_ref[...] * og_ref[...]
    mixed = (_dot(oa_ref[...], woa_ref[...]) + _dot(ob_ref[...], wob_ref[...])
             + _dot(oc.astype(BF16), woc_ref[...]))
    x1 = _layer_norm(alpha * x_ref[...] + mixed, g1_ref[...], b1_ref[...])
    x1_ref[...] = x1

    x_hi, x_lo = _split2(x1)
    wrh = wrh_ref[...]
    lt = _dot_nt(wrh, x_hi) + _dot_nt(wrh, x_lo) + _dot_nt(wrl_ref[...], x_hi) + br_ref[...]
    tm = lt.shape[1]
    coarse = lt[0:N_GROUPS]
    r4 = lax.broadcasted_iota(jnp.int32, (N_GROUPS, tm), 0)
    cmax = jnp.max(coarse, axis=0, keepdims=True)
    group = jnp.min(jnp.where(coarse == cmax, r4, N_GROUPS), axis=0, keepdims=True)
    group_w = 1.0 / jnp.sum(jnp.exp(coarse - cmax), axis=0, keepdims=True)
    fine = jnp.zeros((EXPERTS_PER_GROUP, tm), F32)
    for g in range(N_GROUPS):
        fine = jnp.where(group == g, lt[8 + 8 * g:16 + 8 * g], fine)
    r8 = lax.broadcasted_iota(jnp.int32, (EXPERTS_PER_GROUP, tm), 0)
    v1 = jnp.max(fine, axis=0, keepdims=True)
    i1 = jnp.min(jnp.where(fine == v1, r8, EXPERTS_PER_GROUP), axis=0, keepdims=True)
    rest = jnp.where(r8 == i1, -jnp.inf, fine)
    v2 = jnp.max(rest, axis=0, keepdims=True)
    i2 = jnp.min(jnp.where(rest == v2, r8, EXPERTS_PER_GROUP), axis=0, keepdims=True)
    e2 = jnp.exp(v2 - v1)
    w1 = group_w / (1.0 + e2)
    w2 = group_w * e2 / (1.0 + e2)
    eid_ref[...] = jnp.where(r8 == 0, group * EXPERTS_PER_GROUP + i1,
                             jnp.where(r8 == 1, group * EXPERTS_PER_GROUP + i2, 0))
    gate_ref[...] = jnp.where(r8 == 0, w1, jnp.where(r8 == 1, w2, 0.0))


def _outproj(x, oa, ob, ocf, ocb, og, cg, e96, woa, wob, woc, g1, b1, wrh, wrl, br, alpha):
    n = x.shape[0]
    tm = ROW_TILE
    row = lambda w: pl.BlockSpec((tm, w), lambda i: (i, 0))
    col = pl.BlockSpec((8, tm), lambda i: (0, i))
    return pl.pallas_call(
        functools.partial(_outproj_kernel, alpha=alpha),
        grid=(n // tm,),
        in_specs=[row(D_MODEL), row(A_QW), row(B_VW), row(C_VW), row(C_VW), row(C_VW), _full(cg.shape),
                  _full(e96.shape), _full(woa.shape), _full(wob.shape), _full(woc.shape), _full(g1.shape),
                  _full(b1.shape), _full(wrh.shape), _full(wrl.shape), _full(br.shape)],
        out_specs=[row(D_MODEL), col, col],
        out_shape=[jax.ShapeDtypeStruct((n, D_MODEL), F32), jax.ShapeDtypeStruct((8, n), jnp.int32),
                   jax.ShapeDtypeStruct((8, n), F32)],
        compiler_params=_cparams(("parallel",)),
        name="outproj",
    )(x, oa, ob, ocf, ocb, og, cg, e96, woa, wob, woc, g1, b1, wrh, wrl, br)


def _moe_kernel(be_ref, tok_ref, tokn_ref, dst_ref, gate_ref, x_hbm, wg_ref, wu_ref, wd_ref, y_hbm,
                xbuf, ybuf, gsem, ssem):
    i = pl.program_id(0)
    nb = pl.num_programs(0)
    bm = xbuf.shape[1]
    slot = i % 2

    def gather(idx_ref, sl, start):
        for r in range(bm):
            cp = pltpu.make_async_copy(x_hbm.at[pl.ds(idx_ref[0, 0, r], 1), :], xbuf.at[sl, pl.ds(r, 1), :],
                                       gsem.at[sl])
            if start:
                cp.start()
            else:
                cp.wait()

    def scatter(sl, start):
        for r in range(bm):
            cp = pltpu.make_async_copy(ybuf.at[sl, pl.ds(r, 1), :], y_hbm.at[pl.ds(dst_ref[0, 0, r], 1), :],
                                       ssem.at[sl])
            if start:
                cp.start()
            else:
                cp.wait()

    @pl.when(i == 0)
    def _():
        gather(tok_ref, 0, True)

    @pl.when(i + 1 < nb)
    def _():
        gather(tokn_ref, 1 - slot, True)

    gather(tok_ref, slot, False)

    @pl.when(i >= 2)
    def _():
        scatter(slot, False)

    xb = xbuf[slot].astype(BF16)
    hg = _dot(xb, wg_ref[0])
    hu = _dot(xb, wu_ref[0])
    h = (hg * (1.0 / (1.0 + jnp.exp(-hg))) * hu).astype(BF16)
    ybuf[slot] = _dot(h, wd_ref[0]) * gate_ref[...]
    scatter(slot, True)

    @pl.when(i == nb - 1)
    def _():
        scatter(slot, False)

        @pl.when(nb >= 2)
        def _():
            scatter(1 - slot, False)


def _moe(block_expert, slot_tok, slot_dst, slot_gate, x1, wg, wu, wd, n_rows_out):
    nb = slot_tok.shape[0]
    bm = MOE_BM
    grid_spec = pltpu.PrefetchScalarGridSpec(
        num_scalar_prefetch=1,
        grid=(nb,),
        in_specs=[pl.BlockSpec((1, 1, bm), lambda i, be: (i, 0, 0), memory_space=pltpu.SMEM),
                  pl.BlockSpec((1, 1, bm), lambda i, be: (jnp.minimum(i + 1, nb - 1), 0, 0),
                               memory_space=pltpu.SMEM),
                  pl.BlockSpec((1, 1, bm), lambda i, be: (i, 0, 0), memory_space=pltpu.SMEM),
                  pl.BlockSpec((bm, 1), lambda i, be: (i, 0)),
                  pl.BlockSpec(memory_space=pl.ANY),
                  pl.BlockSpec((1, D_MODEL, D_EXPERT), lambda i, be: (be[i], 0, 0)),
                  pl.BlockSpec((1, D_MODEL, D_EXPERT), lambda i, be: (be[i], 0, 0)),
                  pl.BlockSpec((1, D_EXPERT, D_MODEL), lambda i, be: (be[i], 0, 0))],
        out_specs=pl.BlockSpec(memory_space=pl.ANY),
        scratch_shapes=[pltpu.VMEM((2, bm, D_MODEL), F32), pltpu.VMEM((2, bm, D_MODEL), F32),
                        pltpu.SemaphoreType.DMA((2,)), pltpu.SemaphoreType.DMA((2,))],
    )
    return pl.pallas_call(
        _moe_kernel,
        grid_spec=grid_spec,
        out_shape=jax.ShapeDtypeStruct((n_rows_out, D_MODEL), F32),
        compiler_params=_cparams(("arbitrary",)),
        name="moe_ffn",
    )(block_expert, slot_tok, slot_tok, slot_dst, slot_gate, x1, wg, wu, wd)


def _route_slots(eid, gate, n_tok):
    bm = MOE_BM
    n_assign = 2 * n_tok
    flat_e = eid[0:2].reshape(-1)
    flat_g = gate[0:2].reshape(-1)
    order = jnp.argsort(flat_e).astype(jnp.int32)
    counts = jnp.sum(flat_e[None, :] == jnp.arange(N_EXPERTS, dtype=jnp.int32)[:, None], axis=1).astype(jnp.int32)
    seg_end = jnp.cumsum(counts)
    seg_start = seg_end - counts
    padded = (counts + bm - 1) // bm * bm
    pad_end = jnp.cumsum(padded)
    pad_start = pad_end - padded
    nb = n_assign // bm + N_EXPERTS
    n_slots = nb * bm
    block_start = jnp.arange(nb, dtype=jnp.int32) * bm
    block_expert = jnp.minimum(jnp.sum(block_start[:, None] >= pad_end[None, :], axis=1), N_EXPERTS - 1
                               ).astype(jnp.int32)
    slot = jnp.arange(n_slots, dtype=jnp.int32)
    slot_e = jnp.repeat(block_expert, bm)
    rank = slot - pad_start[slot_e]
    valid = rank < counts[slot_e]
    src = order[jnp.where(valid, seg_start[slot_e] + rank, 0)]
    slot_tok = jnp.where(valid, src % n_tok, 0)
    slot_dst = jnp.where(valid, src, n_assign + slot - seg_end[slot_e])
    slot_gate = jnp.where(valid, flat_g[src], 0.0)
    return (block_expert, slot_tok.reshape(nb, 1, bm), slot_dst.reshape(nb, 1, bm), slot_gate.reshape(n_slots, 1))


def _final_kernel(x1_ref, y0_ref, y1_ref, p_ref, wpg_ref, wpp_ref, g2_ref, b2_ref, o_ref, *, alpha):
    x1 = x1_ref[...]
    gate = 1.0 / (1.0 + jnp.exp(-_dot(x1.astype(BF16), wpg_ref[...])))
    ple = gate * _dot(p_ref[...].astype(BF16), wpp_ref[...])
    o_ref[...] = _layer_norm(alpha * x1 + (y0_ref[...] + y1_ref[...]) + ple, g2_ref[...], b2_ref[...])


def _final(x1, y2, p, wpg, wpp, g2, b2, alpha):
    n = x1.shape[0]
    tm = ROW_TILE
    nt = n // tm
    row = lambda w: pl.BlockSpec((tm, w), lambda i: (i, 0))
    return pl.pallas_call(
        functools.partial(_final_kernel, alpha=alpha),
        grid=(nt,),
        in_specs=[row(D_MODEL), row(D_MODEL), pl.BlockSpec((tm, D_MODEL), lambda i: (i + nt, 0)), row(PLE_DIM),
                  _full(wpg.shape), _full(wpp.shape), _full(g2.shape), _full(b2.shape)],
        out_specs=row(D_MODEL),
        out_shape=jax.ShapeDtypeStruct((n, D_MODEL), F32),
        compiler_params=_cparams(("parallel",)),
        name="final",
    )(x1, y2, y2, p, wpg, wpp, g2, b2)


def _rope_tables(seq_len):
    rows = seq_len // GRID_W
    row = jnp.repeat(jnp.arange(rows, dtype=F32), GRID_W)
    col = jnp.tile(jnp.arange(GRID_W, dtype=F32), rows)
    n_pairs = HEAD_DIM // 4
    inv_freq = ROPE_THETA ** (-jnp.arange(n_pairs, dtype=F32) / n_pairs)
    ang = jnp.concatenate([row[:, None] * inv_freq, col[:, None] * inv_freq], axis=-1)
    cos = jnp.repeat(jnp.cos(ang), 2, axis=-1)
    sin = jnp.repeat(jnp.sin(ang), 2, axis=-1) * jnp.tile(jnp.asarray([-1.0, 1.0], F32), HEAD_DIM // 2)
    return jnp.tile(cos, (1, 2)), jnp.tile(sin, (1, 2))


def _pad_cols(w, width):
    return jnp.pad(w, ((0, 0), (0, width - w.shape[1])))


def kernel(x_prompt, x_sample, p_prompt, p_sample, w_in, a_q_norm, a_k_norm, b_lambda, b_subln, c_gate_w2,
           c_gate_b, c_norm, w_out, ln1_g, ln1_b, w_router_coarse, b_router_coarse, w_router_fine, b_router_fine,
           w_exp_gate, w_exp_up, w_exp_down, w_ple_gate, w_ple_proj, ln2_g, ln2_b):
    depth = w_in.shape[0]
    alpha = (2.0 * depth) ** 0.25
    bp, tp, _ = x_prompt.shape
    bs, ts, _ = x_sample.shape
    n_p, n_s = bp * tp, bs * ts
    n_tok = n_p + n_s
    tm = ROW_TILE
    assert tp % tm == 0 and ts % tm == 0 and tp % GRID_W == 0 and ts % GRID_W == 0
    assert (2 * n_tok) % MOE_BM == 0

    cos_t, sin_t = _rope_tables(max(tp, ts))
    np_tiles, tp_tiles, ts_tiles = n_p // tm, tp // tm, ts // tm

    def pos_block(i):
        return jnp.where(i < np_tiles, i % tp_tiles, (i - np_tiles) % ts_tiles)

    e64 = _block_diag_avg(A_QW, HEAD_DIM)
    e96 = _block_diag_avg(C_VW, C_V_DIM)
    x = jnp.concatenate([x_prompt.reshape(n_p, D_MODEL), x_sample.reshape(n_s, D_MODEL)], axis=0)

    def per_group(t, width):
        return t[:n_p].reshape(bp, tp, width), t[n_p:].reshape(bs, ts, width)

    def transposed(t, width):
        a, b = per_group(t, width)
        return a.transpose(0, 2, 1), b.transpose(0, 2, 1)

    def key_tiled_t(t, n_heads, ks_max):
        def one(v, bsz, seq):
            ks = min(ks_max, seq)
            vt = v.reshape(bsz, seq // ks, ks, n_heads, HEAD_DIM).transpose(0, 1, 3, 4, 2)
            ones = jnp.ones((bsz, seq // ks, n_heads, ATT_VROWS - HEAD_DIM, ks), BF16)
            return jnp.concatenate([vt, ones], axis=3).reshape(bsz, seq // ks, n_heads * ATT_VROWS, ks), ks
        a, b = per_group(t, n_heads * HEAD_DIM)
        return one(a, bp, tp), one(b, bs, ts)

    def merge_t(a, b, width):
        return jnp.concatenate([a.transpose(0, 2, 1).reshape(n_p, width),
                                b.transpose(0, 2, 1).reshape(n_s, width)], axis=0)

    def merge(a, b, width):
        return jnp.concatenate([a.reshape(n_p, width), b.reshape(n_s, width)], axis=0)

    for i in range(depth):
        offs = np.cumsum([0, A_QW, A_KW, A_KW, B_QW, B_QW, B_VW, C_KW, C_KW, C_VW, C_GATE_RANK, C_GATE_RANK, C_VW])
        cols = [w_in[i][:, offs[j]:offs[j + 1]] for j in range(12)]
        wa = jnp.concatenate(cols[0:3], axis=1).astype(BF16)
        wb = jnp.concatenate(cols[3:6], axis=1).astype(BF16)
        wc = jnp.concatenate([_pad_cols(cols[6], C_KP), _pad_cols(cols[7], C_KP), cols[8], cols[11],
                              _pad_cols(jnp.concatenate([cols[9], cols[10]], axis=1), 128)], axis=1).astype(BF16)
        w2 = c_gate_w2[i]
        w2f = jnp.zeros((128, C_KP), F32).at[:C_GATE_RANK, :C_KW].set(w2[0]).astype(BF16)
        w2b = jnp.zeros((128, C_KP), F32).at[C_GATE_RANK:2 * C_GATE_RANK, :C_KW].set(w2[1]).astype(BF16)
        gbf = _pad_cols(c_gate_b[i, 0][None, :], C_KP)
        gbb = _pad_cols(c_gate_b[i, 1][None, :], C_KP)
        gq = jnp.tile(a_q_norm[i], A_HEADS)[None, :]
        gk = jnp.tile(a_k_norm[i], A_KV_HEADS)[None, :]

        (qa, ka, va, qb, kb, vb, cq, ck, cv, gf, gb, og) = _inproj(
            x, wa, wb, wc, w2f, w2b, gbf, gbb, gq, gk, e64, cos_t, sin_t, pos_block)

        qa_p, qa_s = transposed(qa, A_QW)
        ka_p, ka_s = per_group(ka, A_KW)
        (va_p, ks_p), (va_s, ks_s) = key_tiled_t(va, A_KV_HEADS, ATT_KS_A)
        oa = merge_t(_attn_a(qa_p, ka_p, va_p, ks_p), _attn_a(qa_s, ka_s, va_s, ks_s), A_QW)

        lam_init = 0.8 - 0.6 * math.exp(-0.3 * i)
        lv = b_lambda[i].astype(F32)
        lam = (jnp.exp(jnp.sum(lv[0] * lv[1])) - jnp.exp(jnp.sum(lv[2] * lv[3])) + lam_init).reshape(1, 1)
        gain_col = b_subln[i].reshape(B_V_DIM, 1)
        qb_p, qb_s = transposed(qb, B_QW)
        kb_p, kb_s = per_group(kb, B_QW)
        (vb_p, ks_p), (vb_s, ks_s) = key_tiled_t(vb, B_HEADS, ATT_KS_B)
        ob = merge_t(_attn_b(lam, qb_p, kb_p, vb_p, gain_col, 1.0 - lam_init, ks_p),
                     _attn_b(lam, qb_s, kb_s, vb_s, gain_col, 1.0 - lam_init, ks_s), B_VW)

        cq_p, cq_s = per_group(cq, C_KP)
        ck_p, ck_s = per_group(ck, C_KP)
        cv_p, cv_s = per_group(cv, C_VW)
        gf_p, gf_s = per_group(gf, C_KP)
        gb_p, gb_s = per_group(gb, C_KP)
        ocf = merge(_gla(cq_p, ck_p, cv_p, gf_p, False), _gla(cq_s, ck_s, cv_s, gf_s, False), C_VW)
        ocb = merge(_gla(cq_p, ck_p, cv_p, gb_p, True), _gla(cq_s, ck_s, cv_s, gb_s, True), C_VW)

        wo = w_out[i].astype(BF16)
        wr = jnp.zeros((ROUTE_ROWS, D_MODEL), F32)
        wr = wr.at[0:N_GROUPS].set(w_router_coarse[i].T).at[8:].set(w_router_fine[i].T)
        wrh = wr.astype(BF16)
        wrl = (wr - wrh.astype(F32)).astype(BF16)
        br = jnp.zeros((ROUTE_ROWS, 1), F32)
        br = br.at[0:N_GROUPS, 0].set(b_router_coarse[i].astype(F32)).at[8:, 0].set(b_router_fine[i].astype(F32))
        x1, eid, gate = _outproj(
            x, oa, ob, ocf, ocb, og, jnp.tile(c_norm[i], C_HEADS)[None, :], e96,
            wo[:A_QW], wo[A_QW:A_QW + B_VW], wo[A_QW + B_VW:], ln1_g[i][None, :], ln1_b[i][None, :],
            wrh, wrl, br, alpha)

        block_expert, slot_tok, slot_dst, slot_gate = _route_slots(eid, gate, n_tok)
        y2 = _moe(block_expert, slot_tok, slot_dst, slot_gate, x1, w_exp_gate[i].astype(BF16),
                  w_exp_up[i].astype(BF16), w_exp_down[i].astype(BF16), slot_gate.shape[0])

        p_all = jnp.concatenate([p_prompt[i].reshape(n_p, PLE_DIM), p_sample[i].reshape(n_s, PLE_DIM)], axis=0)
        x = _final(x1, y2, p_all, w_ple_gate[i].astype(BF16), w_ple_proj[i].astype(BF16),
                   ln2_g[i][None, :], ln2_b[i][None, :], alpha)

    return x[:n_p].reshape(bp, tp, D_MODEL), x[n_p:].reshape(bs, ts, D_MODEL)
```

```python
import functools
import math

import numpy as np
import jax
import jax.numpy as jnp
from jax import lax
from jax.experimental import pallas as pl
from jax.experimental.pallas import tpu as pltpu

F32 = jnp.float32
BF16 = jnp.bfloat16

D_MODEL = 1024
GRID_W = 64
HEAD_DIM = 64
NORM_EPS = 1e-6
A_HEADS = 6
A_KV_HEADS = 2
A_GROUP = A_HEADS // A_KV_HEADS
ROPE_THETA = 10000.0
B_HEADS = 4
B_QK_DIM = 32
B_V_DIM = 64
B_MAPS = 2 * B_HEADS
C_HEADS = 4
C_K_DIM = 48
C_V_DIM = 96
C_GATE_RANK = 16
C_GATE_TAU = 16.0
C_CHUNK = 64
C_KW = C_HEADS * C_K_DIM
C_KP = 256
C_VW = C_HEADS * C_V_DIM
N_GROUPS = 4
EXPERTS_PER_GROUP = 8
N_EXPERTS = N_GROUPS * EXPERTS_PER_GROUP
D_EXPERT = 512
PLE_DIM = 256
A_QW = A_HEADS * HEAD_DIM
A_KW = A_KV_HEADS * HEAD_DIM
B_QW = B_HEADS * 2 * B_QK_DIM
B_VW = B_HEADS * B_V_DIM
ROUTE_ROWS = 8 + N_EXPERTS

VMEM_LIMIT = 56 * 1024 * 1024

ROW_TILE = 512
ATT_TQ = 512
ATT_COL = 256
ATT_KS_A = 512
ATT_KS_B = 512
ATT_VROWS = HEAD_DIM + 16
LOG2E = math.log2(math.e)
GLA_BLOCK = 512
MOE_BM = 256


def _cparams(sem):
    return pltpu.CompilerParams(dimension_semantics=sem, vmem_limit_bytes=VMEM_LIMIT)


def _full(shape):
    nd = len(shape)
    return pl.BlockSpec(shape, lambda *_: (0,) * nd)


def _dot(a, b):
    return jnp.dot(a, b, preferred_element_type=F32)


def _dot_nt(a, b):
    return lax.dot_general(a, b, (((1,), (1,)), ((), ())), preferred_element_type=F32)


def _dot_tn(a, b):
    return lax.dot_general(a, b, (((0,), (0,)), ((), ())), preferred_element_type=F32)


def _split2(x):
    hi = x.astype(BF16)
    lo = (x - hi.astype(F32)).astype(BF16)
    return hi, lo


def _split3(x):
    hi = x.astype(BF16)
    r = x - hi.astype(F32)
    mid = r.astype(BF16)
    lo = (r - mid.astype(F32)).astype(BF16)
    return hi, mid, lo


def _seg_mean(sq, e):
    hi, lo = _split2(sq)
    return _dot(hi, e) + _dot(lo, e)


def _block_diag_avg(width, seg):
    idx = np.arange(width) // seg
    return jnp.asarray((idx[:, None] == idx[None, :]).astype(np.float32) / seg, dtype=BF16)


def _rope(x, cos, sin_signed):
    pieces = []
    for c in range(x.shape[1] // 128):
        xc = x[:, c * 128:(c + 1) * 128]
        nxt = pltpu.roll(xc, 127, 1)
        prv = pltpu.roll(xc, 1, 1)
        lane = lax.broadcasted_iota(jnp.int32, xc.shape, 1)
        sw = jnp.where(lane % 2 == 0, nxt, prv)
        pieces.append(xc * cos + sw * sin_signed)
    return pieces[0] if len(pieces) == 1 else jnp.concatenate(pieces, axis=1)


def _inproj_kernel(x_ref, wa_ref, wb_ref, wc_ref, w2f_ref, w2b_ref, gbf_ref, gbb_ref, gq_ref, gk_ref,
                   e64_ref, e32_ref, cos_ref, sin_ref,
                   qa_ref, ka_ref, va_ref, qb_ref, kb_ref, vb_ref, cq_ref, ck_ref, cv_ref, gf_ref, gb_ref,
                   og_ref, bn_ref):
    xb = x_ref[...].astype(BF16)
    cos = cos_ref[...]
    sin = sin_ref[...]

    za = _dot(xb, wa_ref[...])
    q = za[:, :A_QW]
    k = za[:, A_QW:A_QW + A_KW]
    e64 = e64_ref[...]
    qn = q * lax.rsqrt(_seg_mean(q * q, e64) + NORM_EPS) * gq_ref[...]
    kn = k * lax.rsqrt(_seg_mean(k * k, e64[:A_KW, :A_KW]) + NORM_EPS) * gk_ref[...]
    qa_ref[...] = (_rope(qn, cos, sin) * (HEAD_DIM ** -0.5 * LOG2E)).astype(BF16)
    ka_ref[...] = _rope(kn, cos, sin).astype(BF16)
    va_ref[...] = za[:, A_QW + A_KW:].astype(BF16)

    zb = _dot(xb, wb_ref[...])
    qb = (zb[:, :B_QW] * (B_QK_DIM ** -0.5 * LOG2E)).astype(BF16)
    kb = zb[:, B_QW:2 * B_QW].astype(BF16)
    qb_ref[...] = qb
    kb_ref[...] = kb
    vb_ref[...] = zb[:, 2 * B_QW:].astype(BF16)

    def max_sqnorm(t):
        tf = t.astype(F32)
        hi, lo = _split2(tf * tf)
        return jnp.max(_dot(hi, e32_ref[...]) + _dot(lo, e32_ref[...]), axis=0, keepdims=True)

    r8 = lax.broadcasted_iota(jnp.int32, (8, 128), 0)
    bn_ref[...] = jnp.where(r8 == 0, max_sqnorm(qb), jnp.where(r8 == 1, max_sqnorm(kb), 0.0))

    zc = _dot(xb, wc_ref[...])
    cq_ref[...] = zc[:, :C_KP] * (C_K_DIM ** -0.5)
    ck_ref[...] = zc[:, C_KP:2 * C_KP]
    cv_ref[...] = zc[:, 2 * C_KP:2 * C_KP + C_VW]
    og = zc[:, 2 * C_KP + C_VW:2 * C_KP + 2 * C_VW]
    og_ref[...] = og * (1.0 / (1.0 + jnp.exp(-og)))
    lr = zc[:, 2 * C_KP + 2 * C_VW:].astype(BF16)

    def log_decay(w2_ref, bias_ref):
        g = _dot(lr, w2_ref[...]) + bias_ref[...]
        return (jnp.minimum(g, 0.0) - jnp.log(1.0 + jnp.exp(-jnp.abs(g)))) * (1.0 / C_GATE_TAU)

    gf_ref[...] = log_decay(w2f_ref, gbf_ref)
    gb_ref[...] = log_decay(w2b_ref, gbb_ref)


def _inproj(x, wa, wb, wc, w2f, w2b, gbf, gbb, gq, gk, e64, e32, cos_t, sin_t, pos_block):
    n = x.shape[0]
    tm = ROW_TILE
    row = lambda w: pl.BlockSpec((tm, w), lambda i: (i, 0))
    tab = pl.BlockSpec((tm, 128), lambda i: (pos_block(i), 0))
    outs = [(A_QW, BF16), (A_KW, BF16), (A_KW, BF16), (B_QW, BF16), (B_QW, BF16), (B_VW, BF16),
            (C_KP, F32), (C_KP, F32), (C_VW, F32), (C_KP, F32), (C_KP, F32), (C_VW, F32)]
    return pl.pallas_call(
        _inproj_kernel,
        grid=(n // tm,),
        in_specs=[row(D_MODEL), _full(wa.shape), _full(wb.shape), _full(wc.shape), _full(w2f.shape),
                  _full(w2b.shape), _full(gbf.shape), _full(gbb.shape), _full(gq.shape), _full(gk.shape),
                  _full(e64.shape), _full(e32.shape), tab, tab],
        out_specs=[row(w) for w, _ in outs] + [pl.BlockSpec((8, 128), lambda i: (i, 0))],
        out_shape=[jax.ShapeDtypeStruct((n, w), dt) for w, dt in outs]
                  + [jax.ShapeDtypeStruct((n // tm * 8, 128), F32)],
        compiler_params=_cparams(("parallel",)),
        name="inproj",
    )(x, wa, wb, wc, w2f, w2b, gbf, gbb, gq, gk, e64, e32, cos_t, sin_t)


def _attn_loop(k_ref, vt_ref, qpad, s_scr, m_ref, acc_ref, ks, vhead_of_map):
    n_maps, _, tq = qpad.shape
    ncol = tq // ATT_COL
    n_sub = k_ref.shape[1] // ks
    m_ref[...] = jnp.full(m_ref.shape, -jnp.inf, F32)
    acc_ref[...] = jnp.zeros(acc_ref.shape, F32)

    k0 = k_ref[0, 0:ks, :]
    for mp in range(n_maps):
        for c in range(ncol):
            s_scr[mp, c] = _dot(k0, qpad[mp, :, c * ATT_COL:(c + 1) * ATT_COL])

    def key_tile(j, carry):
        jn = jnp.minimum(j + 1, n_sub - 1)
        kn = k_ref[0, pl.ds(pl.multiple_of(jn * ks, ks), ks), :]
        for c in range(ncol):
            cs = slice(c * ATT_COL, (c + 1) * ATT_COL)
            for mp in range(n_maps):
                vh = vhead_of_map[mp]
                s = s_scr[mp, c]
                s_scr[mp, c] = _dot(kn, qpad[mp, :, cs])
                m_prev = m_ref[mp, :, cs]
                m_new = jnp.maximum(m_prev, jnp.max(s, axis=0, keepdims=True))
                p = jnp.exp2(s - m_new).astype(BF16)
                vt = vt_ref[0, j, vh * ATT_VROWS:(vh + 1) * ATT_VROWS, :]
                acc_ref[mp, :, cs] = jnp.exp2(m_prev - m_new) * acc_ref[mp, :, cs] + _dot(vt, p)
                m_ref[mp, :, cs] = m_new
        return carry

    lax.fori_loop(0, n_sub, key_tile, 0)


def _attn_out(acc_ref, mp):
    return acc_ref[mp, 0:HEAD_DIM, :] / acc_ref[mp, HEAD_DIM:HEAD_DIM + 1, :]


def _attn_a_kernel(qt_ref, k_ref, vt_ref, ot_ref, qpad, s_scr, m_ref, acc_ref, *, ks):
    qpad[...] = jnp.zeros(qpad.shape, BF16)
    for h in range(A_HEADS):
        g = h // A_GROUP
        qpad[h, g * HEAD_DIM:(g + 1) * HEAD_DIM, :] = qt_ref[0, h * HEAD_DIM:(h + 1) * HEAD_DIM, :]
    _attn_loop(k_ref, vt_ref, qpad, s_scr, m_ref, acc_ref, ks, tuple(h // A_GROUP for h in range(A_HEADS)))
    for h in range(A_HEADS):
        ot_ref[0, h * HEAD_DIM:(h + 1) * HEAD_DIM, :] = _attn_out(acc_ref, h).astype(BF16)


def _attn_scratch(n_maps, qk_width, tq, ks):
    return [pltpu.VMEM((n_maps, qk_width, tq), BF16),
            pltpu.VMEM((n_maps, tq // ATT_COL, ks, ATT_COL), F32),
            pltpu.VMEM((n_maps, 1, tq), F32),
            pltpu.VMEM((n_maps, ATT_VROWS, tq), F32)]


def _attn_a(qt, k, vt, ks):
    bsz, _, seq = qt.shape
    tq = min(ATT_TQ, seq)
    return pl.pallas_call(
        functools.partial(_attn_a_kernel, ks=ks),
        grid=(bsz, seq // tq),
        in_specs=[pl.BlockSpec((1, A_QW, tq), lambda b, qi: (b, 0, qi)),
                  pl.BlockSpec((1, seq, A_KW), lambda b, qi: (b, 0, 0)),
                  pl.BlockSpec((1, seq // ks, A_KV_HEADS * ATT_VROWS, ks), lambda b, qi: (b, 0, 0, 0))],
        out_specs=pl.BlockSpec((1, A_QW, tq), lambda b, qi: (b, 0, qi)),
        out_shape=jax.ShapeDtypeStruct((bsz, A_QW, seq), BF16),
        scratch_shapes=_attn_scratch(A_HEADS, A_KW, tq, ks),
        compiler_params=_cparams(("parallel", "arbitrary")),
        name="attn_a",
    )(qt, k, vt)


def _alibi_slope(h):
    return 2.0 ** (-8.0 * (h + 1) / B_HEADS)


def _attn_b_kernel(bnd_ref, lam_ref, qt_ref, k_ref, vt_ref, tpos_ref, gain_ref, ot_ref, qpad, s_scr, m_ref,
                   acc_ref, *, ks, out_scale):
    b = pl.program_id(0)
    qi = pl.program_id(1)
    tq = qt_ref.shape[2]
    ncol = tq // ATT_COL
    n_sub = k_ref.shape[1] // ks

    qpad[...] = jnp.zeros(qpad.shape, BF16)
    for mp in range(B_MAPS):
        qpad[mp, mp * B_QK_DIM:(mp + 1) * B_QK_DIM, :] = qt_ref[0, mp * B_QK_DIM:(mp + 1) * B_QK_DIM, :]
    m_ref[...] = jnp.full(m_ref.shape, -jnp.inf, F32)
    acc_ref[...] = jnp.zeros(acc_ref.shape, F32)

    def key_rows(j):
        return k_ref[0, pl.ds(pl.multiple_of(j * ks, ks), ks), :]

    for h in range(B_HEADS):
        slope = _alibi_slope(h) * LOG2E
        base = ((b * pl.num_programs(1) + qi) * B_HEADS + h) * 2
        j_lo = bnd_ref[base]
        j_hi = bnd_ref[base + 1]

        k0 = key_rows(j_lo)
        for mi in range(2):
            for c in range(ncol):
                s_scr[mi, c] = _dot(k0, qpad[2 * h + mi, :, c * ATT_COL:(c + 1) * ATT_COL])

        def key_tile(j, carry, side, h=h, slope=slope):
            kn = key_rows(jnp.minimum(j + 1, n_sub - 1))
            vt = vt_ref[0, j, h * ATT_VROWS:(h + 1) * ATT_VROWS, :]
            for c in range(ncol):
                cs = slice(c * ATT_COL, (c + 1) * ATT_COL)
                delta = qi * tq + c * ATT_COL - j * ks
                if side == 0:
                    rel = (lax.broadcasted_iota(jnp.int32, (ks, ATT_COL), 1)
                           - lax.broadcasted_iota(jnp.int32, (ks, ATT_COL), 0) + delta)
                    bias = jnp.abs(rel).astype(F32) * (-slope)
                    kappa = 0.0
                else:
                    kappa = delta.astype(F32) * (-side * slope)
                for mi in range(2):
                    mp = 2 * h + mi
                    if side == 0:
                        s = s_scr[mi, c] + bias
                    elif side == 1:
                        s = s_scr[mi, c] - tpos_ref[h]
                    else:
                        s = s_scr[mi, c] + tpos_ref[h]
                    s_scr[mi, c] = _dot(kn, qpad[mp, :, cs])
                    m_prev = m_ref[mp, :, cs]
                    m_new = jnp.maximum(m_prev, jnp.max(s, axis=0, keepdims=True) + kappa)
                    p = jnp.exp2(s - (m_new - kappa)).astype(BF16)
                    acc_ref[mp, :, cs] = jnp.exp2(m_prev - m_new) * acc_ref[mp, :, cs] + _dot(vt, p)
                    m_ref[mp, :, cs] = m_new
            return carry

        lax.fori_loop(j_lo, qi, functools.partial(key_tile, side=1), 0)
        key_tile(qi, 0, 0)
        lax.fori_loop(qi + 1, j_hi, functools.partial(key_tile, side=-1), 0)

    lam = lam_ref[0, 0]
    for h in range(B_HEADS):
        o = _attn_out(acc_ref, 2 * h) - lam * _attn_out(acc_ref, 2 * h + 1)
        ms = jnp.mean(o * o, axis=0, keepdims=True)
        o = o * lax.rsqrt(ms + NORM_EPS) * gain_ref[...] * out_scale
        ot_ref[0, h * B_V_DIM:(h + 1) * B_V_DIM, :] = o.astype(BF16)


def _alibi_window(bn_tiles, bsz, seq, tq):
    nq = seq // tq
    t = bn_tiles.reshape(bsz, nq, 8, 128)
    q2 = t[:, :, 0, :B_MAPS].reshape(bsz, nq, B_HEADS, 2).max(axis=-1)
    k2 = t[:, :, 1, :B_MAPS].reshape(bsz, nq, B_HEADS, 2).max(axis=-1).max(axis=1)
    u = jnp.sqrt(q2 * k2[:, None, :]) * 1.01
    slope = jnp.asarray([_alibi_slope(h) * LOG2E for h in range(B_HEADS)], F32)
    reach = (2.0 * u + 152.0) / slope
    w = jnp.minimum(jnp.floor((reach + (tq - 1)) / tq), nq).astype(jnp.int32)
    qi = jnp.arange(nq, dtype=jnp.int32)[None, :, None]
    return jnp.stack([jnp.maximum(qi - w, 0), jnp.minimum(qi + w + 1, nq)], axis=-1).reshape(-1)


def _attn_b(bounds, lam, qt, k, vt, tpos, gain_col, out_scale, ks):
    bsz, _, seq = qt.shape
    tq = min(ATT_TQ, seq)
    assert ks == tq
    grid_spec = pltpu.PrefetchScalarGridSpec(
        num_scalar_prefetch=1,
        grid=(bsz, seq // tq),
        in_specs=[pl.BlockSpec((1, 1), lambda b, qi, bnd: (0, 0), memory_space=pltpu.SMEM),
                  pl.BlockSpec((1, B_QW, tq), lambda b, qi, bnd: (b, 0, qi)),
                  pl.BlockSpec((1, seq, B_QW), lambda b, qi, bnd: (b, 0, 0)),
                  pl.BlockSpec((1, seq // ks, B_HEADS * ATT_VROWS, ks), lambda b, qi, bnd: (b, 0, 0, 0)),
                  pl.BlockSpec((B_HEADS, ks, ATT_COL), lambda b, qi, bnd: (0, 0, 0)),
                  pl.BlockSpec((B_V_DIM, 1), lambda b, qi, bnd: (0, 0))],
        out_specs=pl.BlockSpec((1, B_VW, tq), lambda b, qi, bnd: (b, 0, qi)),
        scratch_shapes=[pltpu.VMEM((B_MAPS, B_QW, tq), BF16),
                        pltpu.VMEM((2, tq // ATT_COL, ks, ATT_COL), F32),
                        pltpu.VMEM((B_MAPS, 1, tq), F32),
                        pltpu.VMEM((B_MAPS, ATT_VROWS, tq), F32)],
    )
    return pl.pallas_call(
        functools.partial(_attn_b_kernel, ks=ks, out_scale=out_scale),
        grid_spec=grid_spec,
        out_shape=jax.ShapeDtypeStruct((bsz, B_VW, seq), BF16),
        compiler_params=_cparams(("parallel", "arbitrary")),
        name="attn_b",
    )(bounds, lam, qt, k, vt, tpos, gain_col)


def _gla_masks(reverse):
    j = np.arange(C_CHUNK)
    tri = (j[None, :] >= j[:, None]) if reverse else (j[None, :] <= j[:, None])
    tri = tri.astype(np.float32)
    d = np.arange(C_KP)
    dhead = np.where(d < C_KW, d // C_K_DIM, -1)
    hj = np.arange(C_HEADS * C_CHUNK) // C_CHUNK
    vhead = np.arange(C_VW) // C_V_DIM
    m_stack = (hj[:, None] == dhead[None, :]).astype(np.float32)
    m_p = np.tile(tri, (1, C_HEADS))
    m_v = (hj[:, None] == vhead[None, :]).astype(np.float32)
    m_s = (vhead[:, None] == dhead[None, :]).astype(np.float32)
    return (jnp.asarray(tri, BF16), jnp.asarray(m_stack, F32), jnp.asarray(m_p, F32), jnp.asarray(m_v, F32),
            jnp.asarray(m_s, F32))


def _gla_kernel(q_ref, k_ref, v_ref, g_ref, tri_ref, mstack_ref, mp_ref, mv_ref, ms_ref, o_ref, st_ref, *,
                reverse):
    @pl.when(pl.program_id(1) == 0)
    def _():
        st_ref[...] = jnp.zeros(st_ref.shape, F32)

    n_chunks = q_ref.shape[1] // C_CHUNK
    edge = 0 if reverse else C_CHUNK - 1

    def chunk(ci, carry):
        c = (n_chunks - 1 - ci) if reverse else ci
        rows = pl.ds(pl.multiple_of(c * C_CHUNK, C_CHUNK), C_CHUNK)
        q = q_ref[0, rows, :]
        k = k_ref[0, rows, :]
        v = v_ref[0, rows, :]
        g = g_ref[0, rows, :]
        tri = tri_ref[...]
        g_hi, g_mid, g_lo = _split3(g)
        b = _dot(tri, g_hi) + _dot(tri, g_mid) + _dot(tri, g_lo)
        b_edge = b[edge:edge + 1, :]
        a = (q * jnp.exp(b)).astype(BF16)
        bm = k * jnp.exp(-b)
        khat = (k * jnp.exp(b_edge - b)).astype(BF16)
        bm_stack = (jnp.concatenate([bm] * C_HEADS, axis=0) * mstack_ref[...]).astype(BF16)
        p = (_dot_nt(a, bm_stack) * mp_ref[...]).astype(BF16)
        v_bd = (jnp.concatenate([v] * C_HEADS, axis=0) * mv_ref[...]).astype(BF16)
        st = st_ref[...]
        o_ref[0, rows, :] = _dot(p, v_bd) + _dot_nt(a, st.astype(BF16))
        st_ref[...] = st * jnp.exp(b_edge) + _dot_tn(v.astype(BF16), khat) * ms_ref[...]
        return carry

    lax.fori_loop(0, n_chunks, chunk, 0)


def _gla(q, k, v, g, reverse):
    bsz, seq, _ = q.shape
    blk = min(GLA_BLOCK, seq)
    nblk = seq // blk
    masks = _gla_masks(reverse)
    tmap = (lambda b, t: (b, nblk - 1 - t, 0)) if reverse else (lambda b, t: (b, t, 0))
    return pl.pallas_call(
        functools.partial(_gla_kernel, reverse=reverse),
        grid=(bsz, nblk),
        in_specs=[pl.BlockSpec((1, blk, C_KP), tmap), pl.BlockSpec((1, blk, C_KP), tmap),
                  pl.BlockSpec((1, blk, C_VW), tmap), pl.BlockSpec((1, blk, C_KP), tmap)]
                 + [_full(m.shape) for m in masks],
        out_specs=pl.BlockSpec((1, blk, C_VW), tmap),
        out_shape=jax.ShapeDtypeStruct((bsz, seq, C_VW), F32),
        scratch_shapes=[pltpu.VMEM((C_VW, C_KP), F32)],
        compiler_params=_cparams(("parallel", "arbitrary")),
        name="gla_bwd" if reverse else "gla_fwd",
    )(q, k, v, g, *masks)


def _layer_norm(x, g, b):
    xc = x - jnp.mean(x, axis=-1, keepdims=True)
    var = jnp.mean(xc * xc, axis=-1, keepdims=True)
    return xc * lax.rsqrt(var + NORM_EPS) * g + b


def _outproj_kernel(x_ref, oa_ref, ob_ref, ocf_ref, ocb_ref, og_ref, cg_ref, e96_ref, woa_ref, wob_ref, woc_ref,
                    g1_ref, b1_ref, wrh_ref, wrl_ref, br_ref, x1_ref, eid_ref, gate_ref, hist_ref, *, alpha):
    oc = ocf_ref[...] + ocb_ref[...]
    oc = oc * lax.rsqrt(_seg_mean(oc * oc, e96_ref[...]) + NORM_EPS) * cg_ref[...] * og_ref[...]
    mixed = (_dot(oa_ref[...], woa_ref[...]) + _dot(ob_ref[...], wob_ref[...])
             + _dot(oc.astype(BF16), woc_ref[...]))
    x1 = _layer_norm(alpha * x_ref[...] + mixed, g1_ref[...], b1_ref[...])
    x1_ref[...] = x1

    x_hi, x_lo = _split2(x1)
    wrh = wrh_ref[...]
    lt = _dot_nt(wrh, x_hi) + _dot_nt(wrh, x_lo) + _dot_nt(wrl_ref[...], x_hi) + br_ref[...]
    tm = lt.shape[1]
    coarse = lt[0:N_GROUPS]
    r4 = lax.broadcasted_iota(jnp.int32, (N_GROUPS, tm), 0)
    cmax = jnp.max(coarse, axis=0, keepdims=True)
    group = jnp.min(jnp.where(coarse == cmax, r4, N_GROUPS), axis=0, keepdims=True)
    group_w = 1.0 / jnp.sum(jnp.exp(coarse - cmax), axis=0, keepdims=True)
    fine = jnp.zeros((EXPERTS_PER_GROUP, tm), F32)
    for g in range(N_GROUPS):
        fine = jnp.where(group == g, lt[8 + 8 * g:16 + 8 * g], fine)
    r8 = lax.broadcasted_iota(jnp.int32, (EXPERTS_PER_GROUP, tm), 0)
    v1 = jnp.max(fine, axis=0, keepdims=True)
    i1 = jnp.min(jnp.where(fine == v1, r8, EXPERTS_PER_GROUP), axis=0, keepdims=True)
    rest = jnp.where(r8 == i1, -jnp.inf, fine)
    v2 = jnp.max(rest, axis=0, keepdims=True)
    i2 = jnp.min(jnp.where(rest == v2, r8, EXPERTS_PER_GROUP), axis=0, keepdims=True)
    e2 = jnp.exp(v2 - v1)
    w1 = group_w / (1.0 + e2)
    w2 = group_w * e2 / (1.0 + e2)
    e1 = group * EXPERTS_PER_GROUP + i1
    e2nd = group * EXPERTS_PER_GROUP + i2
    eid_ref[...] = jnp.where(r8 == 0, e1, jnp.where(r8 == 1, e2nd, 0))
    gate_ref[...] = jnp.where(r8 == 0, w1, jnp.where(r8 == 1, w2, 0.0))
    r32 = lax.broadcasted_iota(jnp.int32, (N_EXPERTS, tm), 0)
    hits = jnp.where(r32 == e1, 1.0, 0.0) + jnp.where(r32 == e2nd, 1.0, 0.0)
    hist_ref[...] = jnp.broadcast_to(jnp.sum(hits, axis=1, keepdims=True), (N_EXPERTS, 128)).astype(jnp.int32)


def _outproj(x, oa, ob, ocf, ocb, og, cg, e96, woa, wob, woc, g1, b1, wrh, wrl, br, alpha):
    n = x.shape[0]
    tm = ROW_TILE
    row = lambda w: pl.BlockSpec((tm, w), lambda i: (i, 0))
    col = pl.BlockSpec((8, tm), lambda i: (0, i))
    return pl.pallas_call(
        functools.partial(_outproj_kernel, alpha=alpha),
        grid=(n // tm,),
        in_specs=[row(D_MODEL), row(A_QW), row(B_VW), row(C_VW), row(C_VW), row(C_VW), _full(cg.shape),
                  _full(e96.shape), _full(woa.shape), _full(wob.shape), _full(woc.shape), _full(g1.shape),
                  _full(b1.shape), _full(wrh.shape), _full(wrl.shape), _full(br.shape)],
        out_specs=[row(D_MODEL), col, col, pl.BlockSpec((N_EXPERTS, 128), lambda i: (i, 0))],
        out_shape=[jax.ShapeDtypeStruct((n, D_MODEL), F32), jax.ShapeDtypeStruct((8, n), jnp.int32),
                   jax.ShapeDtypeStruct((8, n), F32),
                   jax.ShapeDtypeStruct((n // tm * N_EXPERTS, 128), jnp.int32)],
        compiler_params=_cparams(("parallel",)),
        name="outproj",
    )(x, oa, ob, ocf, ocb, og, cg, e96, woa, wob, woc, g1, b1, wrh, wrl, br)


def _moe_kernel(be_ref, tok_ref, tokn_ref, dst_ref, gate_ref, x_hbm, wg_ref, wu_ref, wd_ref, y_hbm,
                xbuf, ybuf, gsem, ssem):
    i = pl.program_id(0)
    nb = pl.num_programs(0)
    bm = xbuf.shape[1]
    slot = i % 2

    def gather(idx_ref, sl, start):
        for r in range(bm):
            cp = pltpu.make_async_copy(x_hbm.at[pl.ds(idx_ref[0, 0, r], 1), :], xbuf.at[sl, pl.ds(r, 1), :],
                                       gsem.at[sl])
            if start:
                cp.start()
            else:
                cp.wait()

    def scatter(sl, start):
        for r in range(bm):
            cp = pltpu.make_async_copy(ybuf.at[sl, pl.ds(r, 1), :], y_hbm.at[pl.ds(dst_ref[0, 0, r], 1), :],
                                       ssem.at[sl])
            if start:
                cp.start()
            else:
                cp.wait()

    @pl.when(i == 0)
    def _():
        gather(tok_ref, 0, True)

    @pl.when(i + 1 < nb)
    def _():
        gather(tokn_ref, 1 - slot, True)

    gather(tok_ref, slot, False)

    @pl.when(i >= 2)
    def _():
        scatter(slot, False)

    xb = xbuf[slot].astype(BF16)
    hg = _dot(xb, wg_ref[0])
    hu = _dot(xb, wu_ref[0])
    h = (hg * (1.0 / (1.0 + jnp.exp(-hg))) * hu).astype(BF16)
    ybuf[slot] = _dot(h, wd_ref[0]) * gate_ref[...]
    scatter(slot, True)

    @pl.when(i == nb - 1)
    def _():
        scatter(slot, False)

        @pl.when(nb >= 2)
        def _():
            scatter(1 - slot, False)


def _moe(block_expert, slot_tok, slot_dst, slot_gate, x1, wg, wu, wd, n_rows_out):
    nb = slot_tok.shape[0]
    bm = MOE_BM
    grid_spec = pltpu.PrefetchScalarGridSpec(
        num_scalar_prefetch=1,
        grid=(nb,),
        in_specs=[pl.BlockSpec((1, 1, bm), lambda i, be: (i, 0, 0), memory_space=pltpu.SMEM),
                  pl.BlockSpec((1, 1, bm), lambda i, be: (jnp.minimum(i + 1, nb - 1), 0, 0),
                               memory_space=pltpu.SMEM),
                  pl.BlockSpec((1, 1, bm), lambda i, be: (i, 0, 0), memory_space=pltpu.SMEM),
                  pl.BlockSpec((bm, 1), lambda i, be: (i, 0)),
                  pl.BlockSpec(memory_space=pl.ANY),
                  pl.BlockSpec((1, D_MODEL, D_EXPERT), lambda i, be: (be[i], 0, 0)),
                  pl.BlockSpec((1, D_MODEL, D_EXPERT), lambda i, be: (be[i], 0, 0)),
                  pl.BlockSpec((1, D_EXPERT, D_MODEL), lambda i, be: (be[i], 0, 0))],
        out_specs=pl.BlockSpec(memory_space=pl.ANY),
        scratch_shapes=[pltpu.VMEM((2, bm, D_MODEL), F32), pltpu.VMEM((2, bm, D_MODEL), F32),
                        pltpu.SemaphoreType.DMA((2,)), pltpu.SemaphoreType.DMA((2,))],
    )
    return pl.pallas_call(
        _moe_kernel,
        grid_spec=grid_spec,
        out_shape=jax.ShapeDtypeStruct((n_rows_out, D_MODEL), F32),
        compiler_params=_cparams(("arbitrary",)),
        name="moe_ffn",
    )(block_expert, slot_tok, slot_tok, slot_dst, slot_gate, x1, wg, wu, wd)


def _route_slots(eid, gate, hist, n_tok):
    bm = MOE_BM
    n_assign = 2 * n_tok
    flat_e = eid[0:2].reshape(-1)
    flat_g = gate[0:2].reshape(-1)
    order = jnp.argsort(flat_e).astype(jnp.int32)
    counts = jnp.sum(hist.reshape(-1, N_EXPERTS, 128)[:, :, 0], axis=0)
    seg_end = jnp.cumsum(counts)
    seg_start = seg_end - counts
    padded = (counts + bm - 1) // bm * bm
    pad_end = jnp.cumsum(padded)
    pad_start = pad_end - padded
    nb = n_assign // bm + N_EXPERTS
    n_slots = nb * bm
    block_start = jnp.arange(nb, dtype=jnp.int32) * bm
    block_expert = jnp.minimum(jnp.sum(block_start[:, None] >= pad_end[None, :], axis=1), N_EXPERTS - 1
                               ).astype(jnp.int32)
    slot = jnp.arange(n_slots, dtype=jnp.int32)
    slot_e = jnp.repeat(block_expert, bm)
    rank = slot - pad_start[slot_e]
    valid = rank < counts[slot_e]
    src = order[jnp.where(valid, seg_start[slot_e] + rank, 0)]
    slot_tok = jnp.where(valid, src % n_tok, 0)
    slot_dst = jnp.where(valid, src, n_assign + slot - seg_end[slot_e])
    slot_gate = jnp.where(valid, flat_g[src], 0.0)
    return (block_expert, slot_tok.reshape(nb, 1, bm), slot_dst.reshape(nb, 1, bm), slot_gate.reshape(n_slots, 1))


def _final_kernel(x1_ref, y0_ref, y1_ref, p_ref, wpg_ref, wpp_ref, g2_ref, b2_ref, o_ref, *, alpha):
    x1 = x1_ref[...]
    gate = 1.0 / (1.0 + jnp.exp(-_dot(x1.astype(BF16), wpg_ref[...])))
    ple = gate * _dot(p_ref[...].astype(BF16), wpp_ref[...])
    o_ref[...] = _layer_norm(alpha * x1 + (y0_ref[...] + y1_ref[...]) + ple, g2_ref[...], b2_ref[...])


def _final(x1, y2, p, wpg, wpp, g2, b2, alpha):
    n = x1.shape[0]
    tm = ROW_TILE
    nt = n // tm
    row = lambda w: pl.BlockSpec((tm, w), lambda i: (i, 0))
    return pl.pallas_call(
        functools.partial(_final_kernel, alpha=alpha),
        grid=(nt,),
        in_specs=[row(D_MODEL), row(D_MODEL), pl.BlockSpec((tm, D_MODEL), lambda i: (i + nt, 0)), row(PLE_DIM),
                  _full(wpg.shape), _full(wpp.shape), _full(g2.shape), _full(b2.shape)],
        out_specs=row(D_MODEL),
        out_shape=jax.ShapeDtypeStruct((n, D_MODEL), F32),
        compiler_params=_cparams(("parallel",)),
        name="final",
    )(x1, y2, y2, p, wpg, wpp, g2, b2)


def _rope_tables(seq_len):
    rows = seq_len // GRID_W
    row = jnp.repeat(jnp.arange(rows, dtype=F32), GRID_W)
    col = jnp.tile(jnp.arange(GRID_W, dtype=F32), rows)
    n_pairs = HEAD_DIM // 4
    inv_freq = ROPE_THETA ** (-jnp.arange(n_pairs, dtype=F32) / n_pairs)
    ang = jnp.concatenate([row[:, None] * inv_freq, col[:, None] * inv_freq], axis=-1)
    cos = jnp.repeat(jnp.cos(ang), 2, axis=-1)
    sin = jnp.repeat(jnp.sin(ang), 2, axis=-1) * jnp.tile(jnp.asarray([-1.0, 1.0], F32), HEAD_DIM // 2)
    return jnp.tile(cos, (1, 2)), jnp.tile(sin, (1, 2))


def _pad_cols(w, width):
    return jnp.pad(w, ((0, 0), (0, width - w.shape[1])))


def kernel(x_prompt, x_sample, p_prompt, p_sample, w_in, a_q_norm, a_k_norm, b_lambda, b_subln, c_gate_w2,
           c_gate_b, c_norm, w_out, ln1_g, ln1_b, w_router_coarse, b_router_coarse, w_router_fine, b_router_fine,
           w_exp_gate, w_exp_up, w_exp_down, w_ple_gate, w_ple_proj, ln2_g, ln2_b):
    depth = w_in.shape[0]
    alpha = (2.0 * depth) ** 0.25
    bp, tp, _ = x_prompt.shape
    bs, ts, _ = x_sample.shape
    n_p, n_s = bp * tp, bs * ts
    n_tok = n_p + n_s
    tm = ROW_TILE
    assert tp % tm == 0 and ts % tm == 0 and tp % GRID_W == 0 and ts % GRID_W == 0
    assert (2 * n_tok) % MOE_BM == 0

    cos_t, sin_t = _rope_tables(max(tp, ts))
    np_tiles, tp_tiles, ts_tiles = n_p // tm, tp // tm, ts // tm

    def pos_block(i):
        return jnp.where(i < np_tiles, i % tp_tiles, (i - np_tiles) % ts_tiles)

    e64 = _block_diag_avg(A_QW, HEAD_DIM)
    e96 = _block_diag_avg(C_VW, C_V_DIM)
    e32 = jnp.asarray(np.arange(B_QW)[:, None] // B_QK_DIM == np.arange(128)[None, :], BF16)

    def tpos(ks):
        li = (np.arange(ATT_COL)[None, :] - np.arange(ks)[:, None]).astype(np.float32)
        return jnp.stack([jnp.asarray(li) * jnp.float32(_alibi_slope(h) * LOG2E) for h in range(B_HEADS)])
    x = jnp.concatenate([x_prompt.reshape(n_p, D_MODEL), x_sample.reshape(n_s, D_MODEL)], axis=0)

    def per_group(t, width):
        return t[:n_p].reshape(bp, tp, width), t[n_p:].reshape(bs, ts, width)

    def transposed(t, width):
        a, b = per_group(t, width)
        return a.transpose(0, 2, 1), b.transpose(0, 2, 1)

    def key_tiled_t(t, n_heads, ks_max):
        def one(v, bsz, seq):
            ks = min(ks_max, seq)
            vt = v.reshape(bsz, seq // ks, ks, n_heads, HEAD_DIM).transpose(0, 1, 3, 4, 2)
            ones = jnp.ones((bsz, seq // ks, n_heads, ATT_VROWS - HEAD_DIM, ks), BF16)
            return jnp.concatenate([vt, ones], axis=3).reshape(bsz, seq // ks, n_heads * ATT_VROWS, ks), ks
        a, b = per_group(t, n_heads * HEAD_DIM)
        return one(a, bp, tp), one(b, bs, ts)

    def merge_t(a, b, width):
        return jnp.concatenate([a.transpose(0, 2, 1).reshape(n_p, width),
                                b.transpose(0, 2, 1).reshape(n_s, width)], axis=0)

    def merge(a, b, width):
        return jnp.concatenate([a.reshape(n_p, width), b.reshape(n_s, width)], axis=0)

    for i in range(depth):
        offs = np.cumsum([0, A_QW, A_KW, A_KW, B_QW, B_QW, B_VW, C_KW, C_KW, C_VW, C_GATE_RANK, C_GATE_RANK, C_VW])
        cols = [w_in[i][:, offs[j]:offs[j + 1]] for j in range(12)]
        wa = jnp.concatenate(cols[0:3], axis=1).astype(BF16)
        wb = jnp.concatenate(cols[3:6], axis=1).astype(BF16)
        wc = jnp.concatenate([_pad_cols(cols[6], C_KP), _pad_cols(cols[7], C_KP), cols[8], cols[11],
                              _pad_cols(jnp.concatenate([cols[9], cols[10]], axis=1), 128)], axis=1).astype(BF16)
        w2 = c_gate_w2[i]
        w2f = jnp.zeros((128, C_KP), F32).at[:C_GATE_RANK, :C_KW].set(w2[0]).astype(BF16)
        w2b = jnp.zeros((128, C_KP), F32).at[C_GATE_RANK:2 * C_GATE_RANK, :C_KW].set(w2[1]).astype(BF16)
        gbf = _pad_cols(c_gate_b[i, 0][None, :], C_KP)
        gbb = _pad_cols(c_gate_b[i, 1][None, :], C_KP)
        gq = jnp.tile(a_q_norm[i], A_HEADS)[None, :]
        gk = jnp.tile(a_k_norm[i], A_KV_HEADS)[None, :]

        (qa, ka, va, qb, kb, vb, cq, ck, cv, gf, gb, og, bn) = _inproj(
            x, wa, wb, wc, w2f, w2b, gbf, gbb, gq, gk, e64, e32, cos_t, sin_t, pos_block)

        qa_p, qa_s = transposed(qa, A_QW)
        ka_p, ka_s = per_group(ka, A_KW)
        (va_p, ks_p), (va_s, ks_s) = key_tiled_t(va, A_KV_HEADS, ATT_KS_A)
        oa = merge_t(_attn_a(qa_p, ka_p, va_p, ks_p), _attn_a(qa_s, ka_s, va_s, ks_s), A_QW)

        lam_init = 0.8 - 0.6 * math.exp(-0.3 * i)
        lv = b_lambda[i].astype(F32)
        lam = (jnp.exp(jnp.sum(lv[0] * lv[1])) - jnp.exp(jnp.sum(lv[2] * lv[3])) + lam_init).reshape(1, 1)
        gain_col = b_subln[i].reshape(B_V_DIM, 1)
        qb_p, qb_s = transposed(qb, B_QW)
        kb_p, kb_s = per_group(kb, B_QW)
        (vb_p, ks_p), (vb_s, ks_s) = key_tiled_t(vb, B_HEADS, ATT_KS_B)
        bn_p, bn_s = bn[:np_tiles * 8], bn[np_tiles * 8:]
        ob = merge_t(_attn_b(_alibi_window(bn_p, bp, tp, ks_p), lam, qb_p, kb_p, vb_p, tpos(ks_p), gain_col,
                             1.0 - lam_init, ks_p),
                     _attn_b(_alibi_window(bn_s, bs, ts, ks_s), lam, qb_s, kb_s, vb_s, tpos(ks_s), gain_col,
                             1.0 - lam_init, ks_s), B_VW)

        cq_p, cq_s = per_group(cq, C_KP)
        ck_p, ck_s = per_group(ck, C_KP)
        cv_p, cv_s = per_group(cv, C_VW)
        gf_p, gf_s = per_group(gf, C_KP)
        gb_p, gb_s = per_group(gb, C_KP)
        ocf = merge(_gla(cq_p, ck_p, cv_p, gf_p, False), _gla(cq_s, ck_s, cv_s, gf_s, False), C_VW)
        ocb = merge(_gla(cq_p, ck_p, cv_p, gb_p, True), _gla(cq_s, ck_s, cv_s, gb_s, True), C_VW)

        wo = w_out[i].astype(BF16)
        wr = jnp.zeros((ROUTE_ROWS, D_MODEL), F32)
        wr = wr.at[0:N_GROUPS].set(w_router_coarse[i].T).at[8:].set(w_router_fine[i].T)
        wrh = wr.astype(BF16)
        wrl = (wr - wrh.astype(F32)).astype(BF16)
        br = jnp.zeros((ROUTE_ROWS, 1), F32)
        br = br.at[0:N_GROUPS, 0].set(b_router_coarse[i].astype(F32)).at[8:, 0].set(b_router_fine[i].astype(F32))
        x1, eid, gate, hist = _outproj(
            x, oa, ob, ocf, ocb, og, jnp.tile(c_norm[i], C_HEADS)[None, :], e96,
            wo[:A_QW], wo[A_QW:A_QW + B_VW], wo[A_QW + B_VW:], ln1_g[i][None, :], ln1_b[i][None, :],
            wrh, wrl, br, alpha)

        block_expert, slot_tok, slot_dst, slot_gate = _route_slots(eid, gate, hist, n_tok)
        y2 = _moe(block_expert, slot_tok, slot_dst, slot_gate, x1, w_exp_gate[i].astype(BF16),
                  w_exp_up[i].astype(BF16), w_exp_down[i].astype(BF16), slot_gate.shape[0])

        p_all = jnp.concatenate([p_prompt[i].reshape(n_p, PLE_DIM), p_sample[i].reshape(n_s, PLE_DIM)], axis=0)
        x = _final(x1, y2, p_all, w_ple_gate[i].astype(BF16), w_ple_proj[i].astype(BF16),
                   ln2_g[i][None, :], ln2_b[i][None, :], alpha)

    return x[:n_p].reshape(bp, tp, D_MODEL), x[n_p:].reshape(bs, ts, D_MODEL)
```

```python
import functools
import math

import numpy as np
import jax
import jax.numpy as jnp
from jax import lax
from jax.experimental import pallas as pl
from jax.experimental.pallas import tpu as pltpu

F32 = jnp.float32
BF16 = jnp.bfloat16

D_MODEL = 1024
GRID_W = 64
HEAD_DIM = 64
NORM_EPS = 1e-6
A_HEADS = 6
A_KV_HEADS = 2
A_GROUP = A_HEADS // A_KV_HEADS
ROPE_THETA = 10000.0
B_HEADS = 4
B_QK_DIM = 32
B_V_DIM = 64
B_MAPS = 2 * B_HEADS
C_HEADS = 4
C_K_DIM = 48
C_V_DIM = 96
C_GATE_RANK = 16
C_GATE_TAU = 16.0
C_CHUNK = 64
C_KW = C_HEADS * C_K_DIM
C_KP = 256
C_VW = C_HEADS * C_V_DIM
N_GROUPS = 4
EXPERTS_PER_GROUP = 8
N_EXPERTS = N_GROUPS * EXPERTS_PER_GROUP
D_EXPERT = 512
PLE_DIM = 256
A_QW = A_HEADS * HEAD_DIM
A_KW = A_KV_HEADS * HEAD_DIM
B_QW = B_HEADS * 2 * B_QK_DIM
B_VW = B_HEADS * B_V_DIM
ROUTE_ROWS = 8 + N_EXPERTS

VMEM_LIMIT = 56 * 1024 * 1024

ROW_TILE = 512
ATT_TQ = 512
ATT_COL = 256
ATT_KS_A = 512
ATT_KS_B = 512
ATT_VROWS = HEAD_DIM + 16
LOG2E = math.log2(math.e)
GLA_BLOCK = 512
GLA_PAR = 2
MOE_BM = 256


def _cparams(sem):
    return pltpu.CompilerParams(dimension_semantics=sem, vmem_limit_bytes=VMEM_LIMIT)


def _full(shape):
    nd = len(shape)
    return pl.BlockSpec(shape, lambda *_: (0,) * nd)


def _dot(a, b):
    return jnp.dot(a, b, preferred_element_type=F32)


def _dot_nt(a, b):
    return lax.dot_general(a, b, (((1,), (1,)), ((), ())), preferred_element_type=F32)


def _dot_tn(a, b):
    return lax.dot_general(a, b, (((0,), (0,)), ((), ())), preferred_element_type=F32)


def _split2(x):
    hi = x.astype(BF16)
    lo = (x - hi.astype(F32)).astype(BF16)
    return hi, lo


def _split3(x):
    hi = x.astype(BF16)
    r = x - hi.astype(F32)
    mid = r.astype(BF16)
    lo = (r - mid.astype(F32)).astype(BF16)
    return hi, mid, lo


def _seg_mean(sq, e):
    hi, lo = _split2(sq)
    return _dot(hi, e) + _dot(lo, e)


def _block_diag_avg(width, seg):
    idx = np.arange(width) // seg
    return jnp.asarray((idx[:, None] == idx[None, :]).astype(np.float32) / seg, dtype=BF16)


def _rope(x, cos, sin_signed):
    pieces = []
    for c in range(x.shape[1] // 128):
        xc = x[:, c * 128:(c + 1) * 128]
        nxt = pltpu.roll(xc, 127, 1)
        prv = pltpu.roll(xc, 1, 1)
        lane = lax.broadcasted_iota(jnp.int32, xc.shape, 1)
        sw = jnp.where(lane % 2 == 0, nxt, prv)
        pieces.append(xc * cos + sw * sin_signed)
    return pieces[0] if len(pieces) == 1 else jnp.concatenate(pieces, axis=1)


def _inproj_kernel(x_ref, wa_ref, wb_ref, wc_ref, w2f_ref, w2b_ref, gbf_ref, gbb_ref, gq_ref, gk_ref,
                   e64_ref, e32_ref, cos_ref, sin_ref,
                   qa_ref, ka_ref, va_ref, qb_ref, kb_ref, vb_ref, cq_ref, ck_ref, cv_ref, gf_ref, gb_ref,
                   og_ref, bn_ref):
    xb = x_ref[...].astype(BF16)
    cos = cos_ref[...]
    sin = sin_ref[...]

    za = _dot(xb, wa_ref[...])
    q = za[:, :A_QW]
    k = za[:, A_QW:A_QW + A_KW]
    e64 = e64_ref[...]
    qn = q * lax.rsqrt(_seg_mean(q * q, e64) + NORM_EPS) * gq_ref[...]
    kn = k * lax.rsqrt(_seg_mean(k * k, e64[:A_KW, :A_KW]) + NORM_EPS) * gk_ref[...]
    qa_ref[...] = (_rope(qn, cos, sin) * (HEAD_DIM ** -0.5 * LOG2E)).astype(BF16)
    ka_ref[...] = _rope(kn, cos, sin).astype(BF16)
    va_ref[...] = za[:, A_QW + A_KW:].astype(BF16)

    zb = _dot(xb, wb_ref[...])
    qb = (zb[:, :B_QW] * (B_QK_DIM ** -0.5 * LOG2E)).astype(BF16)
    kb = zb[:, B_QW:2 * B_QW].astype(BF16)
    qb_ref[...] = qb
    kb_ref[...] = kb
    vb_ref[...] = zb[:, 2 * B_QW:].astype(BF16)

    def max_sqnorm(t):
        tf = t.astype(F32)
        hi, lo = _split2(tf * tf)
        return jnp.max(_dot(hi, e32_ref[...]) + _dot(lo, e32_ref[...]), axis=0, keepdims=True)

    r8 = lax.broadcasted_iota(jnp.int32, (8, 128), 0)
    bn_ref[...] = jnp.where(r8 == 0, max_sqnorm(qb), jnp.where(r8 == 1, max_sqnorm(kb), 0.0))

    zc = _dot(xb, wc_ref[...])
    cq_ref[...] = zc[:, :C_KP] * (C_K_DIM ** -0.5)
    ck_ref[...] = zc[:, C_KP:2 * C_KP]
    cv_ref[...] = zc[:, 2 * C_KP:2 * C_KP + C_VW]
    og = zc[:, 2 * C_KP + C_VW:2 * C_KP + 2 * C_VW]
    og_ref[...] = og * (1.0 / (1.0 + jnp.exp(-og)))
    lr = zc[:, 2 * C_KP + 2 * C_VW:].astype(BF16)

    def log_decay(w2_ref, bias_ref):
        g = _dot(lr, w2_ref[...]) + bias_ref[...]
        return (jnp.minimum(g, 0.0) - jnp.log(1.0 + jnp.exp(-jnp.abs(g)))) * (1.0 / C_GATE_TAU)

    gf_ref[...] = log_decay(w2f_ref, gbf_ref)
    gb_ref[...] = log_decay(w2b_ref, gbb_ref)


def _inproj(x, wa, wb, wc, w2f, w2b, gbf, gbb, gq, gk, e64, e32, cos_t, sin_t, pos_block):
    n = x.shape[0]
    tm = ROW_TILE
    row = lambda w: pl.BlockSpec((tm, w), lambda i: (i, 0))
    tab = pl.BlockSpec((tm, 128), lambda i: (pos_block(i), 0))
    outs = [(A_QW, BF16), (A_KW, BF16), (A_KW, BF16), (B_QW, BF16), (B_QW, BF16), (B_VW, BF16),
            (C_KP, F32), (C_KP, F32), (C_VW, F32), (C_KP, F32), (C_KP, F32), (C_VW, F32)]
    return pl.pallas_call(
        _inproj_kernel,
        grid=(n // tm,),
        in_specs=[row(D_MODEL), _full(wa.shape), _full(wb.shape), _full(wc.shape), _full(w2f.shape),
                  _full(w2b.shape), _full(gbf.shape), _full(gbb.shape), _full(gq.shape), _full(gk.shape),
                  _full(e64.shape), _full(e32.shape), tab, tab],
        out_specs=[row(w) for w, _ in outs] + [pl.BlockSpec((8, 128), lambda i: (i, 0))],
        out_shape=[jax.ShapeDtypeStruct((n, w), dt) for w, dt in outs]
                  + [jax.ShapeDtypeStruct((n // tm * 8, 128), F32)],
        compiler_params=_cparams(("parallel",)),
        name="inproj",
    )(x, wa, wb, wc, w2f, w2b, gbf, gbb, gq, gk, e64, e32, cos_t, sin_t)


def _attn_loop(k_ref, vt_ref, qpad, s_scr, m_ref, acc_ref, ks, vhead_of_map):
    n_maps, _, tq = qpad.shape
    ncol = tq // ATT_COL
    n_sub = k_ref.shape[1] // ks
    m_ref[...] = jnp.full(m_ref.shape, -jnp.inf, F32)
    acc_ref[...] = jnp.zeros(acc_ref.shape, F32)

    k0 = k_ref[0, 0:ks, :]
    for mp in range(n_maps):
        for c in range(ncol):
            s_scr[mp, c] = _dot(k0, qpad[mp, :, c * ATT_COL:(c + 1) * ATT_COL])

    def key_tile(j, carry):
        jn = jnp.minimum(j + 1, n_sub - 1)
        kn = k_ref[0, pl.ds(pl.multiple_of(jn * ks, ks), ks), :]
        for c in range(ncol):
            cs = slice(c * ATT_COL, (c + 1) * ATT_COL)
            for mp in range(n_maps):
                vh = vhead_of_map[mp]
                s = s_scr[mp, c]
                s_scr[mp, c] = _dot(kn, qpad[mp, :, cs])
                m_prev = m_ref[mp, :, cs]
                m_new = jnp.maximum(m_prev, jnp.max(s, axis=0, keepdims=True))
                p = jnp.exp2(s - m_new).astype(BF16)
                vt = vt_ref[0, j, vh * ATT_VROWS:(vh + 1) * ATT_VROWS, :]
                acc_ref[mp, :, cs] = jnp.exp2(m_prev - m_new) * acc_ref[mp, :, cs] + _dot(vt, p)
                m_ref[mp, :, cs] = m_new
        return carry

    lax.fori_loop(0, n_sub, key_tile, 0)


def _attn_out(acc_ref, mp):
    return acc_ref[mp, 0:HEAD_DIM, :] / acc_ref[mp, HEAD_DIM:HEAD_DIM + 1, :]


def _attn_a_kernel(qt_ref, k_ref, vt_ref, ot_ref, qpad, s_scr, m_ref, acc_ref, *, ks):
    qpad[...] = jnp.zeros(qpad.shape, BF16)
    for h in range(A_HEADS):
        g = h // A_GROUP
        qpad[h, g * HEAD_DIM:(g + 1) * HEAD_DIM, :] = qt_ref[0, h * HEAD_DIM:(h + 1) * HEAD_DIM, :]
    _attn_loop(k_ref, vt_ref, qpad, s_scr, m_ref, acc_ref, ks, tuple(h // A_GROUP for h in range(A_HEADS)))
    for h in range(A_HEADS):
        ot_ref[0, h * HEAD_DIM:(h + 1) * HEAD_DIM, :] = _attn_out(acc_ref, h).astype(BF16)


def _attn_scratch(n_maps, qk_width, tq, ks):
    return [pltpu.VMEM((n_maps, qk_width, tq), BF16),
            pltpu.VMEM((n_maps, tq // ATT_COL, ks, ATT_COL), F32),
            pltpu.VMEM((n_maps, 1, tq), F32),
            pltpu.VMEM((n_maps, ATT_VROWS, tq), F32)]


def _attn_a(qt, k, b0, vt, ks):
    bsz, _, seq = qt.shape
    tq = min(ATT_TQ, seq)
    return pl.pallas_call(
        functools.partial(_attn_a_kernel, ks=ks),
        grid=(bsz, seq // tq),
        in_specs=[pl.BlockSpec((1, A_QW, tq), lambda b, qi: (b, 0, qi)),
                  pl.BlockSpec((1, seq, A_KW), lambda b, qi: (b + b0, 0, 0)),
                  pl.BlockSpec((1, seq // ks, A_KV_HEADS * ATT_VROWS, ks), lambda b, qi: (b, 0, 0, 0))],
        out_specs=pl.BlockSpec((1, A_QW, tq), lambda b, qi: (b, 0, qi)),
        out_shape=jax.ShapeDtypeStruct((bsz, A_QW, seq), BF16),
        scratch_shapes=_attn_scratch(A_HEADS, A_KW, tq, ks),
        compiler_params=_cparams(("parallel", "arbitrary")),
        name="attn_a",
    )(qt, k, vt)


def _alibi_slope(h):
    return 2.0 ** (-8.0 * (h + 1) / B_HEADS)


def _attn_b_kernel(bnd_ref, lam_ref, qt_ref, k_ref, vt_ref, tpos_ref, gain_ref, ot_ref, qpad, s_scr, m_ref,
                   acc_ref, *, ks, out_scale):
    b = pl.program_id(0)
    qi = pl.program_id(1)
    tq = qt_ref.shape[2]
    ncol = tq // ATT_COL
    n_sub = k_ref.shape[1] // ks

    qpad[...] = jnp.zeros(qpad.shape, BF16)
    for mp in range(B_MAPS):
        qpad[mp, mp * B_QK_DIM:(mp + 1) * B_QK_DIM, :] = qt_ref[0, mp * B_QK_DIM:(mp + 1) * B_QK_DIM, :]
    m_ref[...] = jnp.full(m_ref.shape, -jnp.inf, F32)
    acc_ref[...] = jnp.zeros(acc_ref.shape, F32)

    def key_rows(j):
        return k_ref[0, pl.ds(pl.multiple_of(j * ks, ks), ks), :]

    for h in range(B_HEADS):
        slope = _alibi_slope(h) * LOG2E
        base = ((b * pl.num_programs(1) + qi) * B_HEADS + h) * 2
        j_lo = bnd_ref[base]
        j_hi = bnd_ref[base + 1]

        k0 = key_rows(j_lo)
        for mi in range(2):
            for c in range(ncol):
                s_scr[mi, c] = _dot(k0, qpad[2 * h + mi, :, c * ATT_COL:(c + 1) * ATT_COL])

        def key_tile(j, carry, side, h=h, slope=slope):
            kn = key_rows(jnp.minimum(j + 1, n_sub - 1))
            vt = vt_ref[0, j, h * ATT_VROWS:(h + 1) * ATT_VROWS, :]
            for c in range(ncol):
                cs = slice(c * ATT_COL, (c + 1) * ATT_COL)
                delta = qi * tq + c * ATT_COL - j * ks
                if side == 0:
                    rel = (lax.broadcasted_iota(jnp.int32, (ks, ATT_COL), 1)
                           - lax.broadcasted_iota(jnp.int32, (ks, ATT_COL), 0) + delta)
                    bias = jnp.abs(rel).astype(F32) * (-slope)
                    kappa = 0.0
                else:
                    kappa = delta.astype(F32) * (-side * slope)
                for mi in range(2):
                    mp = 2 * h + mi
                    if side == 0:
                        s = s_scr[mi, c] + bias
                    elif side == 1:
                        s = s_scr[mi, c] - tpos_ref[h]
                    else:
                        s = s_scr[mi, c] + tpos_ref[h]
                    s_scr[mi, c] = _dot(kn, qpad[mp, :, cs])
                    m_prev = m_ref[mp, :, cs]
                    m_new = jnp.maximum(m_prev, jnp.max(s, axis=0, keepdims=True) + kappa)
                    p = jnp.exp2(s - (m_new - kappa)).astype(BF16)
                    acc_ref[mp, :, cs] = jnp.exp2(m_prev - m_new) * acc_ref[mp, :, cs] + _dot(vt, p)
                    m_ref[mp, :, cs] = m_new
            return carry

        lax.fori_loop(j_lo, qi, functools.partial(key_tile, side=1), 0)
        key_tile(qi, 0, 0)
        lax.fori_loop(qi + 1, j_hi, functools.partial(key_tile, side=-1), 0)

    lam = lam_ref[0, 0]
    for h in range(B_HEADS):
        o = _attn_out(acc_ref, 2 * h) - lam * _attn_out(acc_ref, 2 * h + 1)
        ms = jnp.mean(o * o, axis=0, keepdims=True)
        o = o * lax.rsqrt(ms + NORM_EPS) * gain_ref[...] * out_scale
        ot_ref[0, h * B_V_DIM:(h + 1) * B_V_DIM, :] = o.astype(BF16)


def _alibi_window(bn_tiles, bsz, seq, tq):
    nq = seq // tq
    t = bn_tiles.reshape(bsz, nq, 8, 128)
    q2 = t[:, :, 0, :B_MAPS].reshape(bsz, nq, B_HEADS, 2).max(axis=-1)
    k2 = t[:, :, 1, :B_MAPS].reshape(bsz, nq, B_HEADS, 2).max(axis=-1).max(axis=1)
    u = jnp.sqrt(q2 * k2[:, None, :]) * 1.01
    slope = jnp.asarray([_alibi_slope(h) * LOG2E for h in range(B_HEADS)], F32)
    reach = (2.0 * u + 152.0) / slope
    w = jnp.minimum(jnp.floor((reach + (tq - 1)) / tq), nq).astype(jnp.int32)
    qi = jnp.arange(nq, dtype=jnp.int32)[None, :, None]
    return jnp.stack([jnp.maximum(qi - w, 0), jnp.minimum(qi + w + 1, nq)], axis=-1).reshape(-1)


def _attn_b(bounds, lam, qt, k, b0, vt, tpos, gain_col, out_scale, ks):
    bsz, _, seq = qt.shape
    tq = min(ATT_TQ, seq)
    assert ks == tq
    grid_spec = pltpu.PrefetchScalarGridSpec(
        num_scalar_prefetch=1,
        grid=(bsz, seq // tq),
        in_specs=[pl.BlockSpec((1, 1), lambda b, qi, bnd: (0, 0), memory_space=pltpu.SMEM),
                  pl.BlockSpec((1, B_QW, tq), lambda b, qi, bnd: (b, 0, qi)),
                  pl.BlockSpec((1, seq, B_QW), lambda b, qi, bnd: (b + b0, 0, 0)),
                  pl.BlockSpec((1, seq // ks, B_HEADS * ATT_VROWS, ks), lambda b, qi, bnd: (b, 0, 0, 0)),
                  pl.BlockSpec((B_HEADS, ks, ATT_COL), lambda b, qi, bnd: (0, 0, 0)),
                  pl.BlockSpec((B_V_DIM, 1), lambda b, qi, bnd: (0, 0))],
        out_specs=pl.BlockSpec((1, B_VW, tq), lambda b, qi, bnd: (b, 0, qi)),
        scratch_shapes=[pltpu.VMEM((B_MAPS, B_QW, tq), BF16),
                        pltpu.VMEM((2, tq // ATT_COL, ks, ATT_COL), F32),
                        pltpu.VMEM((B_MAPS, 1, tq), F32),
                        pltpu.VMEM((B_MAPS, ATT_VROWS, tq), F32)],
    )
    return pl.pallas_call(
        functools.partial(_attn_b_kernel, ks=ks, out_scale=out_scale),
        grid_spec=grid_spec,
        out_shape=jax.ShapeDtypeStruct((bsz, B_VW, seq), BF16),
        compiler_params=_cparams(("parallel", "arbitrary")),
        name="attn_b",
    )(bounds, lam, qt, k, vt, tpos, gain_col)


def _gla_masks():
    j = np.arange(C_CHUNK)
    tri_f = (j[None, :] <= j[:, None]).astype(np.float32)
    tri_b = (j[None, :] >= j[:, None]).astype(np.float32)
    d = np.arange(C_KP)
    dhead = np.where(d < C_KW, d // C_K_DIM, -1)
    hj = np.arange(C_HEADS * C_CHUNK) // C_CHUNK
    vhead = np.arange(C_VW) // C_V_DIM
    m_stack = (hj[:, None] == dhead[None, :]).astype(np.float32)
    m_v = (hj[:, None] == vhead[None, :]).astype(np.float32)
    m_s = (vhead[:, None] == dhead[None, :]).astype(np.float32)
    return (jnp.asarray(np.stack([tri_f, tri_b]), BF16),
            jnp.asarray(np.stack([np.tile(tri_f, (1, C_HEADS)), np.tile(tri_b, (1, C_HEADS))]), F32),
            jnp.asarray(m_stack, F32), jnp.asarray(m_v, F32), jnp.asarray(m_s, F32))


def _gla_kernel(qf_ref, kf_ref, vf_ref, gf_ref, qb_ref, kb_ref, vb_ref, gb_ref, tri_ref, mp_ref, mstack_ref,
                mv_ref, ms_ref, of_ref, ob_ref, st_ref):
    @pl.when(pl.program_id(1) == 0)
    def _():
        st_ref[...] = jnp.zeros(st_ref.shape, F32)

    nb = qf_ref.shape[0]
    n_chunks = qf_ref.shape[1] // C_CHUNK
    chains = [(d, e) for e in range(nb) for d in range(2)]
    srcs = ((qf_ref, kf_ref, vf_ref, gf_ref, of_ref), (qb_ref, kb_ref, vb_ref, gb_ref, ob_ref))

    def chunk(ci, carry):
        rows = [pl.ds(pl.multiple_of((ci if d == 0 else n_chunks - 1 - ci) * C_CHUNK, C_CHUNK), C_CHUNK)
                for d, _ in chains]
        bs = []
        for (d, e), r in zip(chains, rows):
            g_hi, g_mid, g_lo = _split3(srcs[d][3][e, r, :])
            tri = tri_ref[d]
            bs.append(_dot(tri, g_hi) + _dot(tri, g_mid) + _dot(tri, g_lo))
        aa, khats, ps, edges = [], [], [], []
        for (d, e), r, b in zip(chains, rows, bs):
            edge = C_CHUNK - 1 if d == 0 else 0
            b_edge = b[edge:edge + 1, :]
            q = srcs[d][0][e, r, :]
            k = srcs[d][1][e, r, :]
            a = (q * jnp.exp(b)).astype(BF16)
            bm = k * jnp.exp(-b)
            khats.append((k * jnp.exp(b_edge - b)).astype(BF16))
            bm_stack = (jnp.concatenate([bm] * C_HEADS, axis=0) * mstack_ref[...]).astype(BF16)
            ps.append((_dot_nt(a, bm_stack) * mp_ref[d]).astype(BF16))
            aa.append(a)
            edges.append(jnp.exp(b_edge))
        for ix, ((d, e), r) in enumerate(zip(chains, rows)):
            v = srcs[d][2][e, r, :]
            v_bd = (jnp.concatenate([v] * C_HEADS, axis=0) * mv_ref[...]).astype(BF16)
            st = st_ref[ix]
            srcs[d][4][e, r, :] = _dot(ps[ix], v_bd) + _dot_nt(aa[ix], st.astype(BF16))
            st_ref[ix] = st * edges[ix] + _dot_tn(v.astype(BF16), khats[ix]) * ms_ref[...]
        return carry

    lax.fori_loop(0, n_chunks, chunk, 0)


def _gla(q, k, v, gf, gb, bsz, b0):
    seq = q.shape[1]
    blk = min(GLA_BLOCK, seq)
    nblk = seq // blk
    nb = GLA_PAR if bsz % GLA_PAR == 0 and b0 % GLA_PAR == 0 else 1
    masks = _gla_masks()
    fmap = lambda b, t: (b + b0 // nb, t, 0)
    bmap = lambda b, t: (b + b0 // nb, nblk - 1 - t, 0)
    omap = lambda b, t: (b, t, 0)
    ormap = lambda b, t: (b, nblk - 1 - t, 0)
    kw = lambda m: pl.BlockSpec((nb, blk, C_KP), m)
    vw = lambda m: pl.BlockSpec((nb, blk, C_VW), m)
    return pl.pallas_call(
        _gla_kernel,
        grid=(bsz // nb, nblk),
        in_specs=[kw(fmap), kw(fmap), vw(fmap), kw(fmap), kw(bmap), kw(bmap), vw(bmap), kw(bmap)]
                 + [_full(m.shape) for m in masks],
        out_specs=[vw(omap), vw(ormap)],
        out_shape=[jax.ShapeDtypeStruct((bsz, seq, C_VW), F32)] * 2,
        scratch_shapes=[pltpu.VMEM((2 * nb, C_VW, C_KP), F32)],
        compiler_params=_cparams(("parallel", "arbitrary")),
        name="gla",
    )(q, k, v, gf, q, k, v, gb, *masks)


def _layer_norm(x, g, b):
    xc = x - jnp.mean(x, axis=-1, keepdims=True)
    var = jnp.mean(xc * xc, axis=-1, keepdims=True)
    return xc * lax.rsqrt(var + NORM_EPS) * g + b


def _outproj_kernel(x_ref, oa_ref, ob_ref, ocf_ref, ocb_ref, og_ref, cg_ref, e96_ref, woa_ref, wob_ref, woc_ref,
                    g1_ref, b1_ref, wrh_ref, wrl_ref, br_ref, x1_ref, x1t_ref, eid_ref, gate_ref, hist_ref, *, alpha):
    oc = ocf_ref[...] + ocb_ref[...]
    oc = oc * lax.rsqrt(_seg_mean(oc * oc, e96_ref[...]) + NORM_EPS) * cg_ref[...] * og_ref[...]
    mixed = (_dot(oa_ref[...], woa_ref[...]) + _dot(ob_ref[...], wob_ref[...])
             + _dot(oc.astype(BF16), woc_ref[...]))
    x1 = _layer_norm(alpha * x_ref[...] + mixed, g1_ref[...], b1_ref[...])
    x1_ref[...] = x1
    for c in range(D_MODEL // 128):
        x1t_ref[pl.ds(c, x1.shape[0], stride=8), :] = x1[:, c * 128:(c + 1) * 128]

    x_hi, x_lo = _split2(x1)
    wrh = wrh_ref[...]
    lt = _dot_nt(wrh, x_hi) + _dot_nt(wrh, x_lo) + _dot_nt(wrl_ref[...], x_hi) + br_ref[...]
    tm = lt.shape[1]
    coarse = lt[0:N_GROUPS]
    r4 = lax.broadcasted_iota(jnp.int32, (N_GROUPS, tm), 0)
    cmax = jnp.max(coarse, axis=0, keepdims=True)
    group = jnp.min(jnp.where(coarse == cmax, r4, N_GROUPS), axis=0, keepdims=True)
    group_w = 1.0 / jnp.sum(jnp.exp(coarse - cmax), axis=0, keepdims=True)
    fine = jnp.zeros((EXPERTS_PER_GROUP, tm), F32)
    for g in range(N_GROUPS):
        fine = jnp.where(group == g, lt[8 + 8 * g:16 + 8 * g], fine)
    r8 = lax.broadcasted_iota(jnp.int32, (EXPERTS_PER_GROUP, tm), 0)
    v1 = jnp.max(fine, axis=0, keepdims=True)
    i1 = jnp.min(jnp.where(fine == v1, r8, EXPERTS_PER_GROUP), axis=0, keepdims=True)
    rest = jnp.where(r8 == i1, -jnp.inf, fine)
    v2 = jnp.max(rest, axis=0, keepdims=True)
    i2 = jnp.min(jnp.where(rest == v2, r8, EXPERTS_PER_GROUP), axis=0, keepdims=True)
    e2 = jnp.exp(v2 - v1)
    w1 = group_w / (1.0 + e2)
    w2 = group_w * e2 / (1.0 + e2)
    e1 = group * EXPERTS_PER_GROUP + i1
    e2nd = group * EXPERTS_PER_GROUP + i2
    eid_ref[...] = jnp.where(r8 == 0, e1, jnp.where(r8 == 1, e2nd, 0))
    gate_ref[...] = jnp.where(r8 == 0, w1, jnp.where(r8 == 1, w2, 0.0))
    r32 = lax.broadcasted_iota(jnp.int32, (N_EXPERTS, tm), 0)
    hits = jnp.where(r32 == e1, 1.0, 0.0) + jnp.where(r32 == e2nd, 1.0, 0.0)
    hist_ref[...] = jnp.broadcast_to(jnp.sum(hits, axis=1, keepdims=True), (N_EXPERTS, 128)).astype(jnp.int32)


def _outproj(x, oa, ob, ocf, ocb, og, cg, e96, woa, wob, woc, g1, b1, wrh, wrl, br, alpha):
    n = x.shape[0]
    tm = ROW_TILE
    row = lambda w: pl.BlockSpec((tm, w), lambda i: (i, 0))
    col = pl.BlockSpec((8, tm), lambda i: (0, i))
    return pl.pallas_call(
        functools.partial(_outproj_kernel, alpha=alpha),
        grid=(n // tm,),
        in_specs=[row(D_MODEL), row(A_QW), row(B_VW), row(C_VW), row(C_VW), row(C_VW), _full(cg.shape),
                  _full(e96.shape), _full(woa.shape), _full(wob.shape), _full(woc.shape), _full(g1.shape),
                  _full(b1.shape), _full(wrh.shape), _full(wrl.shape), _full(br.shape)],
        out_specs=[row(D_MODEL), pl.BlockSpec((tm * 8, 128), lambda i: (i, 0)), col, col,
                   pl.BlockSpec((N_EXPERTS, 128), lambda i: (i, 0))],
        out_shape=[jax.ShapeDtypeStruct((n, D_MODEL), F32), jax.ShapeDtypeStruct((n * 8, 128), F32),
                   jax.ShapeDtypeStruct((8, n), jnp.int32),
                   jax.ShapeDtypeStruct((8, n), F32),
                   jax.ShapeDtypeStruct((n // tm * N_EXPERTS, 128), jnp.int32)],
        compiler_params=_cparams(("parallel",)),
        name="outproj",
    )(x, oa, ob, ocf, ocb, og, cg, e96, woa, wob, woc, g1, b1, wrh, wrl, br)


def _moe_kernel(be_ref, tok_ref, tokn_ref, dst_ref, gate_ref, x_hbm, wg_ref, wu_ref, wd_ref, y_hbm,
                xbuf, ybuf, gsem, ssem):
    i = pl.program_id(0)
    nb = pl.num_programs(0)
    bm = xbuf.shape[1] // 8
    slot = i % 2

    def token_rows(t):
        return pl.ds(pl.multiple_of(t * 8, 8), 8)

    def gather(idx_ref, sl, start):
        for r in range(bm):
            cp = pltpu.make_async_copy(x_hbm.at[token_rows(idx_ref[0, 0, r]), :], xbuf.at[sl, pl.ds(r * 8, 8), :],
                                       gsem.at[sl])
            if start:
                cp.start()
            else:
                cp.wait()

    def scatter(sl, start):
        for r in range(bm):
            cp = pltpu.make_async_copy(ybuf.at[sl, pl.ds(r * 8, 8), :], y_hbm.at[token_rows(dst_ref[0, 0, r]), :],
                                       ssem.at[sl])
            if start:
                cp.start()
            else:
                cp.wait()

    @pl.when(i == 0)
    def _():
        gather(tok_ref, 0, True)

    @pl.when(i >= 2)
    def _():
        scatter(slot, False)

    gather(tok_ref, slot, False)
    gather(tokn_ref, 1 - slot, True)

    xb = jnp.concatenate([xbuf[slot, pl.ds(c, bm, stride=8), :] for c in range(D_MODEL // 128)], axis=1).astype(BF16)
    hg = _dot(xb, wg_ref[0])
    hu = _dot(xb, wu_ref[0])
    h = (hg * (1.0 / (1.0 + jnp.exp(-hg))) * hu).astype(BF16)
    y = _dot(h, wd_ref[0]) * gate_ref[...]
    for c in range(D_MODEL // 128):
        ybuf[slot, pl.ds(c, bm, stride=8), :] = y[:, c * 128:(c + 1) * 128]
    scatter(slot, True)

    @pl.when(i == nb - 1)
    def _():
        scatter(slot, False)
        gather(tok_ref, 1 - slot, False)

        @pl.when(nb >= 2)
        def _():
            scatter(1 - slot, False)


def _moe(block_expert, slot_tok, slot_dst, slot_gate, x1, wg, wu, wd, n_rows_out):
    nb = slot_tok.shape[0]
    bm = MOE_BM
    grid_spec = pltpu.PrefetchScalarGridSpec(
        num_scalar_prefetch=1,
        grid=(nb,),
        in_specs=[pl.BlockSpec((1, 1, bm), lambda i, be: (i, 0, 0), memory_space=pltpu.SMEM),
                  pl.BlockSpec((1, 1, bm), lambda i, be: (jnp.minimum(i + 1, nb - 1), 0, 0),
                               memory_space=pltpu.SMEM),
                  pl.BlockSpec((1, 1, bm), lambda i, be: (i, 0, 0), memory_space=pltpu.SMEM),
                  pl.BlockSpec((bm, 1), lambda i, be: (i, 0)),
                  pl.BlockSpec(memory_space=pl.ANY),
                  pl.BlockSpec((1, D_MODEL, D_EXPERT), lambda i, be: (be[i], 0, 0)),
                  pl.BlockSpec((1, D_MODEL, D_EXPERT), lambda i, be: (be[i], 0, 0)),
                  pl.BlockSpec((1, D_EXPERT, D_MODEL), lambda i, be: (be[i], 0, 0))],
        out_specs=pl.BlockSpec(memory_space=pl.ANY),
        scratch_shapes=[pltpu.VMEM((2, bm * 8, 128), F32), pltpu.VMEM((2, bm * 8, 128), F32),
                        pltpu.SemaphoreType.DMA((2,)), pltpu.SemaphoreType.DMA((2,))],
    )
    return pl.pallas_call(
        _moe_kernel,
        grid_spec=grid_spec,
        out_shape=jax.ShapeDtypeStruct((n_rows_out * 8, 128), F32),
        compiler_params=_cparams(("arbitrary",)),
        name="moe_ffn",
    )(block_expert, slot_tok, slot_tok, slot_dst, slot_gate, x1, wg, wu, wd)


def _route_slots(eid, gate, hist, n_tok):
    bm = MOE_BM
    n_assign = 2 * n_tok
    flat_e = eid[0:2].reshape(-1)
    flat_g = gate[0:2].reshape(-1)
    order = jnp.argsort(flat_e).astype(jnp.int32)
    counts = jnp.sum(hist.reshape(-1, N_EXPERTS, 128)[:, :, 0], axis=0)
    seg_end = jnp.cumsum(counts)
    seg_start = seg_end - counts
    padded = (counts + bm - 1) // bm * bm
    pad_end = jnp.cumsum(padded)
    pad_start = pad_end - padded
    nb = n_assign // bm + N_EXPERTS
    n_slots = nb * bm
    block_start = jnp.arange(nb, dtype=jnp.int32) * bm
    block_expert = jnp.minimum(jnp.sum(block_start[:, None] >= pad_end[None, :], axis=1), N_EXPERTS - 1
                               ).astype(jnp.int32)
    slot = block_start[:, None] + jnp.arange(bm, dtype=jnp.int32)[None, :]
    rank = slot - pad_start[block_expert][:, None]
    valid = rank < counts[block_expert][:, None]
    src = order[jnp.where(valid, seg_start[block_expert][:, None] + rank, 0)]
    slot_tok = jnp.where(valid, src % n_tok, 0)
    slot_dst = jnp.where(valid, src, n_assign + slot - seg_end[block_expert][:, None])
    slot_gate = jnp.where(valid, flat_g[src], 0.0)
    return (block_expert, slot_tok.reshape(nb, 1, bm), slot_dst.reshape(nb, 1, bm), slot_gate.reshape(n_slots, 1))


def _final_kernel(x1_ref, y0_ref, y1_ref, p_ref, wpg_ref, wpp_ref, g2_ref, b2_ref, o_ref, *, alpha):
    x1 = x1_ref[...]
    gate = 1.0 / (1.0 + jnp.exp(-_dot(x1.astype(BF16), wpg_ref[...])))
    ple = gate * _dot(p_ref[...].astype(BF16), wpp_ref[...])
    tm = x1.shape[0]
    ffn = jnp.concatenate([y0_ref[pl.ds(c, tm, stride=8), :] + y1_ref[pl.ds(c, tm, stride=8), :]
                           for c in range(D_MODEL // 128)], axis=1)
    o_ref[...] = _layer_norm(alpha * x1 + ffn + ple, g2_ref[...], b2_ref[...])


def _final(x1, y2, p, wpg, wpp, g2, b2, alpha):
    n = x1.shape[0]
    tm = ROW_TILE
    nt = n // tm
    row = lambda w: pl.BlockSpec((tm, w), lambda i: (i, 0))
    return pl.pallas_call(
        functools.partial(_final_kernel, alpha=alpha),
        grid=(nt,),
        in_specs=[row(D_MODEL), pl.BlockSpec((tm * 8, 128), lambda i: (i, 0)),
                  pl.BlockSpec((tm * 8, 128), lambda i: (i + nt, 0)), row(PLE_DIM),
                  _full(wpg.shape), _full(wpp.shape), _full(g2.shape), _full(b2.shape)],
        out_specs=row(D_MODEL),
        out_shape=jax.ShapeDtypeStruct((n, D_MODEL), F32),
        compiler_params=_cparams(("parallel",)),
        name="final",
    )(x1, y2, y2, p, wpg, wpp, g2, b2)


def _rope_tables(seq_len):
    rows = seq_len // GRID_W
    row = jnp.repeat(jnp.arange(rows, dtype=F32), GRID_W)
    col = jnp.tile(jnp.arange(GRID_W, dtype=F32), rows)
    n_pairs = HEAD_DIM // 4
    inv_freq = ROPE_THETA ** (-jnp.arange(n_pairs, dtype=F32) / n_pairs)
    ang = jnp.concatenate([row[:, None] * inv_freq, col[:, None] * inv_freq], axis=-1)
    cos = jnp.repeat(jnp.cos(ang), 2, axis=-1)
    sin = jnp.repeat(jnp.sin(ang), 2, axis=-1) * jnp.tile(jnp.asarray([-1.0, 1.0], F32), HEAD_DIM // 2)
    return jnp.tile(cos, (1, 2)), jnp.tile(sin, (1, 2))


def _pad_cols(w, width):
    return jnp.pad(w, ((0, 0), (0, width - w.shape[1])))


def kernel(x_prompt, x_sample, p_prompt, p_sample, w_in, a_q_norm, a_k_norm, b_lambda, b_subln, c_gate_w2,
           c_gate_b, c_norm, w_out, ln1_g, ln1_b, w_router_coarse, b_router_coarse, w_router_fine, b_router_fine,
           w_exp_gate, w_exp_up, w_exp_down, w_ple_gate, w_ple_proj, ln2_g, ln2_b):
    depth = w_in.shape[0]
    alpha = (2.0 * depth) ** 0.25
    bp, tp, _ = x_prompt.shape
    bs, ts, _ = x_sample.shape
    n_p, n_s = bp * tp, bs * ts
    n_tok = n_p + n_s
    tm = ROW_TILE
    assert tp % tm == 0 and ts % tm == 0 and tp % GRID_W == 0 and ts % GRID_W == 0
    assert (2 * n_tok) % MOE_BM == 0

    cos_t, sin_t = _rope_tables(max(tp, ts))
    np_tiles, tp_tiles, ts_tiles = n_p // tm, tp // tm, ts // tm

    def pos_block(i):
        return jnp.where(i < np_tiles, i % tp_tiles, (i - np_tiles) % ts_tiles)

    e64 = _block_diag_avg(A_QW, HEAD_DIM)
    e96 = _block_diag_avg(C_VW, C_V_DIM)
    e32 = jnp.asarray(np.arange(B_QW)[:, None] // B_QK_DIM == np.arange(128)[None, :], BF16)

    def tpos(ks):
        li = (np.arange(ATT_COL)[None, :] - np.arange(ks)[:, None]).astype(np.float32)
        return jnp.stack([jnp.asarray(li) * jnp.float32(_alibi_slope(h) * LOG2E) for h in range(B_HEADS)])
    x = jnp.concatenate([x_prompt.reshape(n_p, D_MODEL), x_sample.reshape(n_s, D_MODEL)], axis=0)

    def per_group(t, width):
        return t[:n_p].reshape(bp, tp, width), t[n_p:].reshape(bs, ts, width)

    def group_views(t, width):
        def view(seq, first_row, bsz):
            if n_tok % seq == 0 and first_row % seq == 0:
                return t.reshape(n_tok // seq, seq, width), first_row // seq
            return t[first_row:first_row + bsz * seq].reshape(bsz, seq, width), 0
        return view(tp, 0, bp), view(ts, n_p, bs)

    def transposed(t, width):
        a, b = per_group(t, width)
        return a.transpose(0, 2, 1), b.transpose(0, 2, 1)

    def key_tiled_t(t, n_heads, ks_max):
        def one(v, bsz, seq):
            ks = min(ks_max, seq)
            vt = v.reshape(bsz, seq // ks, ks, n_heads, HEAD_DIM).transpose(0, 1, 3, 4, 2)
            ones = jnp.ones((bsz, seq // ks, n_heads, ATT_VROWS - HEAD_DIM, ks), BF16)
            return jnp.concatenate([vt, ones], axis=3).reshape(bsz, seq // ks, n_heads * ATT_VROWS, ks), ks
        a, b = per_group(t, n_heads * HEAD_DIM)
        return one(a, bp, tp), one(b, bs, ts)

    def merge_t(a, b, width):
        return jnp.concatenate([a.transpose(0, 2, 1).reshape(n_p, width),
                                b.transpose(0, 2, 1).reshape(n_s, width)], axis=0)

    def merge(a, b, width):
        return jnp.concatenate([a.reshape(n_p, width), b.reshape(n_s, width)], axis=0)

    for i in range(depth):
        offs = np.cumsum([0, A_QW, A_KW, A_KW, B_QW, B_QW, B_VW, C_KW, C_KW, C_VW, C_GATE_RANK, C_GATE_RANK, C_VW])
        cols = [w_in[i][:, offs[j]:offs[j + 1]] for j in range(12)]
        wa = jnp.concatenate(cols[0:3], axis=1).astype(BF16)
        wb = jnp.concatenate(cols[3:6], axis=1).astype(BF16)
        wc = jnp.concatenate([_pad_cols(cols[6], C_KP), _pad_cols(cols[7], C_KP), cols[8], cols[11],
                              _pad_cols(jnp.concatenate([cols[9], cols[10]], axis=1), 128)], axis=1).astype(BF16)
        w2 = c_gate_w2[i]
        w2f = jnp.zeros((128, C_KP), F32).at[:C_GATE_RANK, :C_KW].set(w2[0]).astype(BF16)
        w2b = jnp.zeros((128, C_KP), F32).at[C_GATE_RANK:2 * C_GATE_RANK, :C_KW].set(w2[1]).astype(BF16)
        gbf = _pad_cols(c_gate_b[i, 0][None, :], C_KP)
        gbb = _pad_cols(c_gate_b[i, 1][None, :], C_KP)
        gq = jnp.tile(a_q_norm[i], A_HEADS)[None, :]
        gk = jnp.tile(a_k_norm[i], A_KV_HEADS)[None, :]

        (qa, ka, va, qb, kb, vb, cq, ck, cv, gf, gb, og, bn) = _inproj(
            x, wa, wb, wc, w2f, w2b, gbf, gbb, gq, gk, e64, e32, cos_t, sin_t, pos_block)

        qa_p, qa_s = transposed(qa, A_QW)
        (ka_p, kp0), (ka_s, ks0) = group_views(ka, A_KW)
        (va_p, ks_p), (va_s, ks_s) = key_tiled_t(va, A_KV_HEADS, ATT_KS_A)
        oa = merge_t(_attn_a(qa_p, ka_p, kp0, va_p, ks_p), _attn_a(qa_s, ka_s, ks0, va_s, ks_s), A_QW)

        lam_init = 0.8 - 0.6 * math.exp(-0.3 * i)
        lv = b_lambda[i].astype(F32)
        lam = (jnp.exp(jnp.sum(lv[0] * lv[1])) - jnp.exp(jnp.sum(lv[2] * lv[3])) + lam_init).reshape(1, 1)
        gain_col = b_subln[i].reshape(B_V_DIM, 1)
        qb_p, qb_s = transposed(qb, B_QW)
        (kb_p, kp0), (kb_s, ks0) = group_views(kb, B_QW)
        (vb_p, ks_p), (vb_s, ks_s) = key_tiled_t(vb, B_HEADS, ATT_KS_B)
        bn_p, bn_s = bn[:np_tiles * 8], bn[np_tiles * 8:]
        ob = merge_t(_attn_b(_alibi_window(bn_p, bp, tp, ks_p), lam, qb_p, kb_p, kp0, vb_p, tpos(ks_p), gain_col,
                             1.0 - lam_init, ks_p),
                     _attn_b(_alibi_window(bn_s, bs, ts, ks_s), lam, qb_s, kb_s, ks0, vb_s, tpos(ks_s), gain_col,
                             1.0 - lam_init, ks_s), B_VW)

        (cq_p, cp0), (cq_s, cs0) = group_views(cq, C_KP)
        (ck_p, _), (ck_s, _) = group_views(ck, C_KP)
        (cv_p, _), (cv_s, _) = group_views(cv, C_VW)
        (gf_p, _), (gf_s, _) = group_views(gf, C_KP)
        (gb_p, _), (gb_s, _) = group_views(gb, C_KP)
        ocf_p, ocb_p = _gla(cq_p, ck_p, cv_p, gf_p, gb_p, bp, cp0)
        ocf_s, ocb_s = _gla(cq_s, ck_s, cv_s, gf_s, gb_s, bs, cs0)
        ocf = merge(ocf_p, ocf_s, C_VW)
        ocb = merge(ocb_p, ocb_s, C_VW)

        wo = w_out[i].astype(BF16)
        wr = jnp.zeros((ROUTE_ROWS, D_MODEL), F32)
        wr = wr.at[0:N_GROUPS].set(w_router_coarse[i].T).at[8:].set(w_router_fine[i].T)
        wrh = wr.astype(BF16)
        wrl = (wr - wrh.astype(F32)).astype(BF16)
        br = jnp.zeros((ROUTE_ROWS, 1), F32)
        br = br.at[0:N_GROUPS, 0].set(b_router_coarse[i].astype(F32)).at[8:, 0].set(b_router_fine[i].astype(F32))
        x1, x1t, eid, gate, hist = _outproj(
            x, oa, ob, ocf, ocb, og, jnp.tile(c_norm[i], C_HEADS)[None, :], e96,
            wo[:A_QW], wo[A_QW:A_QW + B_VW], wo[A_QW + B_VW:], ln1_g[i][None, :], ln1_b[i][None, :],
            wrh, wrl, br, alpha)

        block_expert, slot_tok, slot_dst, slot_gate = _route_slots(eid, gate, hist, n_tok)
        y2 = _moe(block_expert, slot_tok, slot_dst, slot_gate, x1t, w_exp_gate[i].astype(BF16),
                  w_exp_up[i].astype(BF16), w_exp_down[i].astype(BF16), slot_gate.shape[0])

        p_all = jnp.concatenate([p_prompt[i].reshape(n_p, PLE_DIM), p_sample[i].reshape(n_s, PLE_DIM)], axis=0)
        x = _final(x1, y2, p_all, w_ple_gate[i].astype(BF16), w_ple_proj[i].astype(BF16),
                   ln2_g[i][None, :], ln2_b[i][None, :], alpha)

    return x[:n_p].reshape(bp, tp, D_MODEL), x[n_p:].reshape(bs, ts, D_MODEL)
```

```python
import functools
import math

import numpy as np
import jax
import jax.numpy as jnp
from jax import lax
from jax.experimental import pallas as pl
from jax.experimental.pallas import tpu as pltpu

F32 = jnp.float32
BF16 = jnp.bfloat16

D_MODEL = 1024
GRID_W = 64
HEAD_DIM = 64
NORM_EPS = 1e-6
A_HEADS = 6
A_KV_HEADS = 2
A_GROUP = A_HEADS // A_KV_HEADS
ROPE_THETA = 10000.0
B_HEADS = 4
B_QK_DIM = 32
B_V_DIM = 64
B_MAPS = 2 * B_HEADS
C_HEADS = 4
C_K_DIM = 48
C_V_DIM = 96
C_GATE_RANK = 16
C_GATE_TAU = 16.0
C_CHUNK = 64
C_KW = C_HEADS * C_K_DIM
C_KP = 256
C_VW = C_HEADS * C_V_DIM
N_GROUPS = 4
EXPERTS_PER_GROUP = 8
N_EXPERTS = N_GROUPS * EXPERTS_PER_GROUP
D_EXPERT = 512
PLE_DIM = 256
A_QW = A_HEADS * HEAD_DIM
A_KW = A_KV_HEADS * HEAD_DIM
B_QW = B_HEADS * 2 * B_QK_DIM
B_VW = B_HEADS * B_V_DIM
ROUTE_ROWS = 8 + N_EXPERTS

VMEM_LIMIT = 56 * 1024 * 1024

ROW_TILE = 512
ATT_TQ = 1024
ATT_COL = 256
ATT_KS_A = 512
ATT_KS_B = 512
ATT_VROWS = HEAD_DIM + 16
LOG2E = math.log2(math.e)
GLA_BLOCK = 512
GLA_PAR = 2
MOE_BM = 256


def _cparams(sem):
    return pltpu.CompilerParams(dimension_semantics=sem, vmem_limit_bytes=VMEM_LIMIT)


def _full(shape):
    nd = len(shape)
    return pl.BlockSpec(shape, lambda *_: (0,) * nd)


def _dot(a, b):
    return jnp.dot(a, b, preferred_element_type=F32)


def _dot_nt(a, b):
    return lax.dot_general(a, b, (((1,), (1,)), ((), ())), preferred_element_type=F32)


def _dot_tn(a, b):
    return lax.dot_general(a, b, (((0,), (0,)), ((), ())), preferred_element_type=F32)


def _split2(x):
    hi = x.astype(BF16)
    lo = (x - hi.astype(F32)).astype(BF16)
    return hi, lo


def _split3(x):
    hi = x.astype(BF16)
    r = x - hi.astype(F32)
    mid = r.astype(BF16)
    lo = (r - mid.astype(F32)).astype(BF16)
    return hi, mid, lo


def _seg_mean(sq, e):
    hi, lo = _split2(sq)
    return _dot(hi, e) + _dot(lo, e)


def _block_diag_avg(width, seg):
    idx = np.arange(width) // seg
    return jnp.asarray((idx[:, None] == idx[None, :]).astype(np.float32) / seg, dtype=BF16)


def _rope(x, cos, sin_signed):
    pieces = []
    for c in range(x.shape[1] // 128):
        xc = x[:, c * 128:(c + 1) * 128]
        nxt = pltpu.roll(xc, 127, 1)
        prv = pltpu.roll(xc, 1, 1)
        lane = lax.broadcasted_iota(jnp.int32, xc.shape, 1)
        sw = jnp.where(lane % 2 == 0, nxt, prv)
        pieces.append(xc * cos + sw * sin_signed)
    return pieces[0] if len(pieces) == 1 else jnp.concatenate(pieces, axis=1)


def _inproj_kernel(x_ref, wa_ref, wb_ref, wc_ref, w2f_ref, w2b_ref, gbf_ref, gbb_ref, gq_ref, gk_ref,
                   e64_ref, e32_ref, cos_ref, sin_ref,
                   qa_ref, ka_ref, va_ref, qb_ref, kb_ref, vb_ref, cq_ref, ck_ref, cv_ref, gf_ref, gb_ref,
                   og_ref, bn_ref):
    xb = x_ref[...].astype(BF16)
    cos = cos_ref[...]
    sin = sin_ref[...]

    za = _dot(xb, wa_ref[...])
    q = za[:, :A_QW]
    k = za[:, A_QW:A_QW + A_KW]
    e64 = e64_ref[...]
    qn = q * lax.rsqrt(_seg_mean(q * q, e64) + NORM_EPS) * gq_ref[...]
    kn = k * lax.rsqrt(_seg_mean(k * k, e64[:A_KW, :A_KW]) + NORM_EPS) * gk_ref[...]
    qa_ref[...] = (_rope(qn, cos, sin) * (HEAD_DIM ** -0.5 * LOG2E)).astype(BF16)
    ka_ref[...] = _rope(kn, cos, sin).astype(BF16)
    va_ref[...] = za[:, A_QW + A_KW:].astype(BF16)

    zb = _dot(xb, wb_ref[...])
    qb = (zb[:, :B_QW] * (B_QK_DIM ** -0.5 * LOG2E)).astype(BF16)
    kb = zb[:, B_QW:2 * B_QW].astype(BF16)
    qb_ref[...] = qb
    kb_ref[...] = kb
    vb_ref[...] = zb[:, 2 * B_QW:].astype(BF16)

    def max_sqnorm(t):
        tf = t.astype(F32)
        hi, lo = _split2(tf * tf)
        return jnp.max(_dot(hi, e32_ref[...]) + _dot(lo, e32_ref[...]), axis=0, keepdims=True)

    r8 = lax.broadcasted_iota(jnp.int32, (8, 128), 0)
    bn_ref[...] = jnp.where(r8 == 0, max_sqnorm(qb), jnp.where(r8 == 1, max_sqnorm(kb), 0.0))

    zc = _dot(xb, wc_ref[...])
    cq_ref[...] = zc[:, :C_KP] * (C_K_DIM ** -0.5)
    ck_ref[...] = zc[:, C_KP:2 * C_KP]
    cv_ref[...] = zc[:, 2 * C_KP:2 * C_KP + C_VW]
    og = zc[:, 2 * C_KP + C_VW:2 * C_KP + 2 * C_VW]
    og_ref[...] = og * (1.0 / (1.0 + jnp.exp(-og)))
    lr = zc[:, 2 * C_KP + 2 * C_VW:].astype(BF16)

    def log_decay(w2_ref, bias_ref):
        g = _dot(lr, w2_ref[...]) + bias_ref[...]
        return (jnp.minimum(g, 0.0) - jnp.log(1.0 + jnp.exp(-jnp.abs(g)))) * (1.0 / C_GATE_TAU)

    gf_ref[...] = log_decay(w2f_ref, gbf_ref)
    gb_ref[...] = log_decay(w2b_ref, gbb_ref)


def _inproj(x, wa, wb, wc, w2f, w2b, gbf, gbb, gq, gk, e64, e32, cos_t, sin_t, pos_block):
    n = x.shape[0]
    tm = ROW_TILE
    row = lambda w: pl.BlockSpec((tm, w), lambda i: (i, 0))
    tab = pl.BlockSpec((tm, 128), lambda i: (pos_block(i), 0))
    outs = [(A_QW, BF16), (A_KW, BF16), (A_KW, BF16), (B_QW, BF16), (B_QW, BF16), (B_VW, BF16),
            (C_KP, F32), (C_KP, F32), (C_VW, F32), (C_KP, F32), (C_KP, F32), (C_VW, F32)]
    return pl.pallas_call(
        _inproj_kernel,
        grid=(n // tm,),
        in_specs=[row(D_MODEL), _full(wa.shape), _full(wb.shape), _full(wc.shape), _full(w2f.shape),
                  _full(w2b.shape), _full(gbf.shape), _full(gbb.shape), _full(gq.shape), _full(gk.shape),
                  _full(e64.shape), _full(e32.shape), tab, tab],
        out_specs=[row(w) for w, _ in outs] + [pl.BlockSpec((8, 128), lambda i: (i, 0))],
        out_shape=[jax.ShapeDtypeStruct((n, w), dt) for w, dt in outs]
                  + [jax.ShapeDtypeStruct((n // tm * 8, 128), F32)],
        compiler_params=_cparams(("parallel",)),
        name="inproj",
    )(x, wa, wb, wc, w2f, w2b, gbf, gbb, gq, gk, e64, e32, cos_t, sin_t)


def _attn_loop(k_ref, vt_ref, qpad, s_scr, m_ref, acc_ref, ks, vhead_of_map):
    n_maps, _, tq = qpad.shape
    ncol = tq // ATT_COL
    n_sub = k_ref.shape[1] // ks
    m_ref[...] = jnp.full(m_ref.shape, -jnp.inf, F32)
    acc_ref[...] = jnp.zeros(acc_ref.shape, F32)

    k0 = k_ref[0, 0:ks, :]
    for mp in range(n_maps):
        for c in range(ncol):
            s_scr[mp, c] = _dot(k0, qpad[mp, :, c * ATT_COL:(c + 1) * ATT_COL])

    def key_tile(j, carry):
        jn = jnp.minimum(j + 1, n_sub - 1)
        kn = k_ref[0, pl.ds(pl.multiple_of(jn * ks, ks), ks), :]
        for c in range(ncol):
            cs = slice(c * ATT_COL, (c + 1) * ATT_COL)
            for mp in range(n_maps):
                vh = vhead_of_map[mp]
                s = s_scr[mp, c]
                s_scr[mp, c] = _dot(kn, qpad[mp, :, cs])
                m_prev = m_ref[mp, :, cs]
                m_new = jnp.maximum(m_prev, jnp.max(s, axis=0, keepdims=True))
                p = jnp.exp2(s - m_new).astype(BF16)
                vt = vt_ref[0, j, vh * ATT_VROWS:(vh + 1) * ATT_VROWS, :]
                acc_ref[mp, :, cs] = jnp.exp2(m_prev - m_new) * acc_ref[mp, :, cs] + _dot(vt, p)
                m_ref[mp, :, cs] = m_new
        return carry

    lax.fori_loop(0, n_sub, key_tile, 0)


def _attn_out(acc_ref, mp):
    return acc_ref[mp, 0:HEAD_DIM, :] / acc_ref[mp, HEAD_DIM:HEAD_DIM + 1, :]


def _attn_a_kernel(qt_ref, k_ref, vt_ref, ot_ref, qpad, s_scr, m_ref, acc_ref, *, ks):
    qpad[...] = jnp.zeros(qpad.shape, BF16)
    for h in range(A_HEADS):
        g = h // A_GROUP
        qpad[h, g * HEAD_DIM:(g + 1) * HEAD_DIM, :] = qt_ref[0, h * HEAD_DIM:(h + 1) * HEAD_DIM, :]
    _attn_loop(k_ref, vt_ref, qpad, s_scr, m_ref, acc_ref, ks, tuple(h // A_GROUP for h in range(A_HEADS)))
    for h in range(A_HEADS):
        ot_ref[0, h * HEAD_DIM:(h + 1) * HEAD_DIM, :] = _attn_out(acc_ref, h).astype(BF16)


def _attn_scratch(n_maps, qk_width, tq, ks):
    return [pltpu.VMEM((n_maps, qk_width, tq), BF16),
            pltpu.VMEM((n_maps, tq // ATT_COL, ks, ATT_COL), F32),
            pltpu.VMEM((n_maps, 1, tq), F32),
            pltpu.VMEM((n_maps, ATT_VROWS, tq), F32)]


def _attn_a(qt, k, b0, vt, ks):
    bsz, _, seq = qt.shape
    tq = min(ATT_TQ, seq)
    return pl.pallas_call(
        functools.partial(_attn_a_kernel, ks=ks),
        grid=(bsz, seq // tq),
        in_specs=[pl.BlockSpec((1, A_QW, tq), lambda b, qi: (b, 0, qi)),
                  pl.BlockSpec((1, seq, A_KW), lambda b, qi: (b + b0, 0, 0), pipeline_mode=pl.Buffered(1)),
                  pl.BlockSpec((1, seq // ks, A_KV_HEADS * ATT_VROWS, ks), lambda b, qi: (b, 0, 0, 0),
                               pipeline_mode=pl.Buffered(1))],
        out_specs=pl.BlockSpec((1, A_QW, tq), lambda b, qi: (b, 0, qi)),
        out_shape=jax.ShapeDtypeStruct((bsz, A_QW, seq), BF16),
        scratch_shapes=_attn_scratch(A_HEADS, A_KW, tq, ks),
        compiler_params=_cparams(("parallel", "arbitrary")),
        name="attn_a",
    )(qt, k, vt)


def _alibi_slope(h):
    return 2.0 ** (-8.0 * (h + 1) / B_HEADS)


def _attn_b_kernel(bnd_ref, lam_ref, qt_ref, k_ref, vt_ref, tpos_ref, gain_ref, ot_ref, qpad, s_scr, m_ref,
                   acc_ref, *, ks, out_scale):
    b = pl.program_id(0)
    qi = pl.program_id(1)
    tq = qt_ref.shape[2]
    ncol = tq // ATT_COL
    n_sub = k_ref.shape[1] // ks
    n_diag = tq // ks
    d_lo = qi * n_diag

    qpad[...] = jnp.zeros(qpad.shape, BF16)
    for mp in range(B_MAPS):
        qpad[mp, mp * B_QK_DIM:(mp + 1) * B_QK_DIM, :] = qt_ref[0, mp * B_QK_DIM:(mp + 1) * B_QK_DIM, :]
    m_ref[...] = jnp.full(m_ref.shape, -jnp.inf, F32)
    acc_ref[...] = jnp.zeros(acc_ref.shape, F32)

    def key_rows(j):
        return k_ref[0, pl.ds(pl.multiple_of(j * ks, ks), ks), :]

    for h in range(B_HEADS):
        slope = _alibi_slope(h) * LOG2E
        base = ((b * pl.num_programs(1) + qi) * B_HEADS + h) * 2
        j_lo = bnd_ref[base]
        j_hi = bnd_ref[base + 1]

        k0 = key_rows(j_lo)
        for mi in range(2):
            for c in range(ncol):
                s_scr[mi, c] = _dot(k0, qpad[2 * h + mi, :, c * ATT_COL:(c + 1) * ATT_COL])

        def key_tile(j, carry, side, h=h, slope=slope):
            kn = key_rows(jnp.minimum(j + 1, n_sub - 1))
            vt = vt_ref[0, j, h * ATT_VROWS:(h + 1) * ATT_VROWS, :]
            for c in range(ncol):
                cs = slice(c * ATT_COL, (c + 1) * ATT_COL)
                delta = qi * tq + c * ATT_COL - j * ks
                if side == 0:
                    rel = (lax.broadcasted_iota(jnp.int32, (ks, ATT_COL), 1)
                           - lax.broadcasted_iota(jnp.int32, (ks, ATT_COL), 0) + delta)
                    bias = jnp.abs(rel).astype(F32) * (-slope)
                    kappa = 0.0
                else:
                    kappa = delta.astype(F32) * (-side * slope)
                for mi in range(2):
                    mp = 2 * h + mi
                    if side == 0:
                        s = s_scr[mi, c] + bias
                    elif side == 1:
                        s = s_scr[mi, c] - tpos_ref[h]
                    else:
                        s = s_scr[mi, c] + tpos_ref[h]
                    s_scr[mi, c] = _dot(kn, qpad[mp, :, cs])
                    m_prev = m_ref[mp, :, cs]
                    m_new = jnp.maximum(m_prev, jnp.max(s, axis=0, keepdims=True) + kappa)
                    p = jnp.exp2(s - (m_new - kappa)).astype(BF16)
                    acc_ref[mp, :, cs] = jnp.exp2(m_prev - m_new) * acc_ref[mp, :, cs] + _dot(vt, p)
                    m_ref[mp, :, cs] = m_new
            return carry

        lax.fori_loop(j_lo, d_lo, functools.partial(key_tile, side=1), 0)
        for dj in range(n_diag):
            key_tile(d_lo + dj, 0, 0)
        lax.fori_loop(d_lo + n_diag, j_hi, functools.partial(key_tile, side=-1), 0)

    lam = lam_ref[0, 0]
    for h in range(B_HEADS):
        o = _attn_out(acc_ref, 2 * h) - lam * _attn_out(acc_ref, 2 * h + 1)
        ms = jnp.mean(o * o, axis=0, keepdims=True)
        o = o * lax.rsqrt(ms + NORM_EPS) * gain_ref[...] * out_scale
        ot_ref[0, h * B_V_DIM:(h + 1) * B_V_DIM, :] = o.astype(BF16)


def _alibi_window(bn_tiles, bsz, seq, tq, ks):
    nq = seq // tq
    n_sub = seq // ks
    t = bn_tiles.reshape(bsz, nq, -1, 8, 128)
    q2 = t[:, :, :, 0, :B_MAPS].max(axis=2).reshape(bsz, nq, B_HEADS, 2).max(axis=-1)
    k2 = t[:, :, :, 1, :B_MAPS].max(axis=(1, 2)).reshape(bsz, B_HEADS, 2).max(axis=-1)
    u = jnp.sqrt(q2 * k2[:, None, :]) * 1.01
    slope = jnp.asarray([_alibi_slope(h) * LOG2E for h in range(B_HEADS)], F32)
    reach = (2.0 * u + 152.0) / slope
    w = jnp.minimum(jnp.floor((reach + (ks - 1)) / ks), n_sub).astype(jnp.int32)
    d_lo = jnp.arange(nq, dtype=jnp.int32)[None, :, None] * (tq // ks)
    return jnp.stack([jnp.maximum(d_lo - w, 0), jnp.minimum(d_lo + tq // ks + w, n_sub)], axis=-1).reshape(-1)


def _attn_b(bounds, lam, qt, k, b0, vt, tpos, gain_col, out_scale, ks):
    bsz, _, seq = qt.shape
    tq = min(ATT_TQ, seq)
    assert tq % ks == 0
    grid_spec = pltpu.PrefetchScalarGridSpec(
        num_scalar_prefetch=1,
        grid=(bsz, seq // tq),
        in_specs=[pl.BlockSpec((1, 1), lambda b, qi, bnd: (0, 0), memory_space=pltpu.SMEM),
                  pl.BlockSpec((1, B_QW, tq), lambda b, qi, bnd: (b, 0, qi)),
                  pl.BlockSpec((1, seq, B_QW), lambda b, qi, bnd: (b + b0, 0, 0), pipeline_mode=pl.Buffered(1)),
                  pl.BlockSpec((1, seq // ks, B_HEADS * ATT_VROWS, ks), lambda b, qi, bnd: (b, 0, 0, 0),
                               pipeline_mode=pl.Buffered(1)),
                  pl.BlockSpec((B_HEADS, ks, ATT_COL), lambda b, qi, bnd: (0, 0, 0), pipeline_mode=pl.Buffered(1)),
                  pl.BlockSpec((B_V_DIM, 1), lambda b, qi, bnd: (0, 0))],
        out_specs=pl.BlockSpec((1, B_VW, tq), lambda b, qi, bnd: (b, 0, qi)),
        scratch_shapes=[pltpu.VMEM((B_MAPS, B_QW, tq), BF16),
                        pltpu.VMEM((2, tq // ATT_COL, ks, ATT_COL), F32),
                        pltpu.VMEM((B_MAPS, 1, tq), F32),
                        pltpu.VMEM((B_MAPS, ATT_VROWS, tq), F32)],
    )
    return pl.pallas_call(
        functools.partial(_attn_b_kernel, ks=ks, out_scale=out_scale),
        grid_spec=grid_spec,
        out_shape=jax.ShapeDtypeStruct((bsz, B_VW, seq), BF16),
        compiler_params=_cparams(("parallel", "arbitrary")),
        name="attn_b",
    )(bounds, lam, qt, k, vt, tpos, gain_col)


def _gla_masks():
    j = np.arange(C_CHUNK)
    tri_f = (j[None, :] <= j[:, None]).astype(np.float32)
    tri_b = (j[None, :] >= j[:, None]).astype(np.float32)
    d = np.arange(C_KP)
    dhead = np.where(d < C_KW, d // C_K_DIM, -1)
    hj = np.arange(C_HEADS * C_CHUNK) // C_CHUNK
    vhead = np.arange(C_VW) // C_V_DIM
    m_stack = (hj[:, None] == dhead[None, :]).astype(np.float32)
    m_v = (hj[:, None] == vhead[None, :]).astype(np.float32)
    m_s = (vhead[:, None] == dhead[None, :]).astype(np.float32)
    return (jnp.asarray(np.stack([tri_f, tri_b]), BF16),
            jnp.asarray(np.stack([np.tile(tri_f, (1, C_HEADS)), np.tile(tri_b, (1, C_HEADS))]), F32),
            jnp.asarray(m_stack, F32), jnp.asarray(m_v, F32), jnp.asarray(m_s, F32))


def _gla_kernel(qf_ref, kf_ref, vf_ref, gf_ref, qb_ref, kb_ref, vb_ref, gb_ref, tri_ref, mp_ref, mstack_ref,
                mv_ref, ms_ref, of_ref, ob_ref, st_ref):
    @pl.when(pl.program_id(1) == 0)
    def _():
        st_ref[...] = jnp.zeros(st_ref.shape, F32)

    nb = qf_ref.shape[0]
    n_chunks = qf_ref.shape[1] // C_CHUNK
    chains = [(d, e) for e in range(nb) for d in range(2)]
    srcs = ((qf_ref, kf_ref, vf_ref, gf_ref, of_ref), (qb_ref, kb_ref, vb_ref, gb_ref, ob_ref))

    def chunk(ci, carry):
        rows = [pl.ds(pl.multiple_of((ci if d == 0 else n_chunks - 1 - ci) * C_CHUNK, C_CHUNK), C_CHUNK)
                for d, _ in chains]
        bs = []
        for (d, e), r in zip(chains, rows):
            g_hi, g_mid, g_lo = _split3(srcs[d][3][e, r, :])
            tri = tri_ref[d]
            bs.append(_dot(tri, g_hi) + _dot(tri, g_mid) + _dot(tri, g_lo))
        aa, khats, ps, edges = [], [], [], []
        for (d, e), r, b in zip(chains, rows, bs):
            edge = C_CHUNK - 1 if d == 0 else 0
            b_edge = b[edge:edge + 1, :]
            q = srcs[d][0][e, r, :]
            k = srcs[d][1][e, r, :]
            a = (q * jnp.exp(b)).astype(BF16)
            bm = k * jnp.exp(-b)
            khats.append((k * jnp.exp(b_edge - b)).astype(BF16))
            bm_stack = (jnp.concatenate([bm] * C_HEADS, axis=0) * mstack_ref[...]).astype(BF16)
            ps.append((_dot_nt(a, bm_stack) * mp_ref[d]).astype(BF16))
            aa.append(a)
            edges.append(jnp.exp(b_edge))
        for ix, ((d, e), r) in enumerate(zip(chains, rows)):
            v = srcs[d][2][e, r, :]
            v_bd = (jnp.concatenate([v] * C_HEADS, axis=0) * mv_ref[...]).astype(BF16)
            st = st_ref[ix]
            srcs[d][4][e, r, :] = _dot(ps[ix], v_bd) + _dot_nt(aa[ix], st.astype(BF16))
            st_ref[ix] = st * edges[ix] + _dot_tn(v.astype(BF16), khats[ix]) * ms_ref[...]
        return carry

    lax.fori_loop(0, n_chunks, chunk, 0)


def _gla(q, k, v, gf, gb, bsz, b0):
    seq = q.shape[1]
    blk = min(GLA_BLOCK, seq)
    nblk = seq // blk
    nb = GLA_PAR if bsz % GLA_PAR == 0 and b0 % GLA_PAR == 0 else 1
    masks = _gla_masks()
    fmap = lambda b, t: (b + b0 // nb, t, 0)
    bmap = lambda b, t: (b + b0 // nb, nblk - 1 - t, 0)
    omap = lambda b, t: (b, t, 0)
    ormap = lambda b, t: (b, nblk - 1 - t, 0)
    kw = lambda m: pl.BlockSpec((nb, blk, C_KP), m)
    vw = lambda m: pl.BlockSpec((nb, blk, C_VW), m)
    return pl.pallas_call(
        _gla_kernel,
        grid=(bsz // nb, nblk),
        in_specs=[kw(fmap), kw(fmap), vw(fmap), kw(fmap), kw(bmap), kw(bmap), vw(bmap), kw(bmap)]
                 + [_full(m.shape) for m in masks],
        out_specs=[vw(omap), vw(ormap)],
        out_shape=[jax.ShapeDtypeStruct((bsz, seq, C_VW), F32)] * 2,
        scratch_shapes=[pltpu.VMEM((2 * nb, C_VW, C_KP), F32)],
        compiler_params=_cparams(("parallel", "arbitrary")),
        name="gla",
    )(q, k, v, gf, q, k, v, gb, *masks)


def _layer_norm(x, g, b):
    xc = x - jnp.mean(x, axis=-1, keepdims=True)
    var = jnp.mean(xc * xc, axis=-1, keepdims=True)
    return xc * lax.rsqrt(var + NORM_EPS) * g + b


def _outproj_kernel(x_ref, oa_ref, ob_ref, ocf_ref, ocb_ref, og_ref, cg_ref, e96_ref, woa_ref, wob_ref, woc_ref,
                    g1_ref, b1_ref, wrh_ref, wrl_ref, br_ref, x1_ref, x1t_ref, eid_ref, gate_ref, hist_ref, *, alpha):
    oc = ocf_ref[...] + ocb_ref[...]
    oc = oc * lax.rsqrt(_seg_mean(oc * oc, e96_ref[...]) + NORM_EPS) * cg_ref[...] * og_ref[...]
    mixed = (_dot(oa_ref[...], woa_ref[...]) + _dot(ob_ref[...], wob_ref[...])
             + _dot(oc.astype(BF16), woc_ref[...]))
    x1 = _layer_norm(alpha * x_ref[...] + mixed, g1_ref[...], b1_ref[...])
    x1_ref[...] = x1
    for c in range(D_MODEL // 128):
        x1t_ref[pl.ds(c, x1.shape[0], stride=8), :] = x1[:, c * 128:(c + 1) * 128]

    x_hi, x_lo = _split2(x1)
    wrh = wrh_ref[...]
    lt = _dot_nt(wrh, x_hi) + _dot_nt(wrh, x_lo) + _dot_nt(wrl_ref[...], x_hi) + br_ref[...]
    tm = lt.shape[1]
    coarse = lt[0:N_GROUPS]
    r4 = lax.broadcasted_iota(jnp.int32, (N_GROUPS, tm), 0)
    cmax = jnp.max(coarse, axis=0, keepdims=True)
    group = jnp.min(jnp.where(coarse == cmax, r4, N_GROUPS), axis=0, keepdims=True)
    group_w = 1.0 / jnp.sum(jnp.exp(coarse - cmax), axis=0, keepdims=True)
    fine = jnp.zeros((EXPERTS_PER_GROUP, tm), F32)
    for g in range(N_GROUPS):
        fine = jnp.where(group == g, lt[8 + 8 * g:16 + 8 * g], fine)
    r8 = lax.broadcasted_iota(jnp.int32, (EXPERTS_PER_GROUP, tm), 0)
    v1 = jnp.max(fine, axis=0, keepdims=True)
    i1 = jnp.min(jnp.where(fine == v1, r8, EXPERTS_PER_GROUP), axis=0, keepdims=True)
    rest = jnp.where(r8 == i1, -jnp.inf, fine)
    v2 = jnp.max(rest, axis=0, keepdims=True)
    i2 = jnp.min(jnp.where(rest == v2, r8, EXPERTS_PER_GROUP), axis=0, keepdims=True)
    e2 = jnp.exp(v2 - v1)
    w1 = group_w / (1.0 + e2)
    w2 = group_w * e2 / (1.0 + e2)
    e1 = group * EXPERTS_PER_GROUP + i1
    e2nd = group * EXPERTS_PER_GROUP + i2
    eid_ref[...] = jnp.where(r8 == 0, e1, jnp.where(r8 == 1, e2nd, 0))
    gate_ref[...] = jnp.where(r8 == 0, w1, jnp.where(r8 == 1, w2, 0.0))
    r32 = lax.broadcasted_iota(jnp.int32, (N_EXPERTS, tm), 0)
    hits = jnp.where(r32 == e1, 1.0, 0.0) + jnp.where(r32 == e2nd, 1.0, 0.0)
    hist_ref[...] = jnp.broadcast_to(jnp.sum(hits, axis=1, keepdims=True), (N_EXPERTS, 128)).astype(jnp.int32)


def _outproj(x, oa, ob, ocf, ocb, og, cg, e96, woa, wob, woc, g1, b1, wrh, wrl, br, alpha):
    n = x.shape[0]
    tm = ROW_TILE
    row = lambda w: pl.BlockSpec((tm, w), lambda i: (i, 0))
    col = pl.BlockSpec((8, tm), lambda i: (0, i))
    return pl.pallas_call(
        functools.partial(_outproj_kernel, alpha=alpha),
        grid=(n // tm,),
        in_specs=[row(D_MODEL), row(A_QW), row(B_VW), row(C_VW), row(C_VW), row(C_VW), _full(cg.shape),
                  _full(e96.shape), _full(woa.shape), _full(wob.shape), _full(woc.shape), _full(g1.shape),
                  _full(b1.shape), _full(wrh.shape), _full(wrl.shape), _full(br.shape)],
        out_specs=[row(D_MODEL), pl.BlockSpec((tm * 8, 128), lambda i: (i, 0)), col, col,
                   pl.BlockSpec((N_EXPERTS, 128), lambda i: (i, 0))],
        out_shape=[jax.ShapeDtypeStruct((n, D_MODEL), F32), jax.ShapeDtypeStruct((n * 8, 128), F32),
                   jax.ShapeDtypeStruct((8, n), jnp.int32),
                   jax.ShapeDtypeStruct((8, n), F32),
                   jax.ShapeDtypeStruct((n // tm * N_EXPERTS, 128), jnp.int32)],
        compiler_params=_cparams(("parallel",)),
        name="outproj",
    )(x, oa, ob, ocf, ocb, og, cg, e96, woa, wob, woc, g1, b1, wrh, wrl, br)


def _moe_kernel(be_ref, tok_ref, tokn_ref, dst_ref, gate_ref, x_hbm, wg_ref, wu_ref, wd_ref, y_hbm,
                xbuf, ybuf, gsem, ssem):
    i = pl.program_id(0)
    nb = pl.num_programs(0)
    bm = xbuf.shape[1] // 8
    slot = i % 2

    def token_rows(t):
        return pl.ds(pl.multiple_of(t * 8, 8), 8)

    def gather(idx_ref, sl, start):
        for r in range(bm):
            cp = pltpu.make_async_copy(x_hbm.at[token_rows(idx_ref[0, 0, r]), :], xbuf.at[sl, pl.ds(r * 8, 8), :],
                                       gsem.at[sl])
            if start:
                cp.start()
            else:
                cp.wait()

    def scatter(sl, start):
        for r in range(bm):
            cp = pltpu.make_async_copy(ybuf.at[sl, pl.ds(r * 8, 8), :], y_hbm.at[token_rows(dst_ref[0, 0, r]), :],
                                       ssem.at[sl])
            if start:
                cp.start()
            else:
                cp.wait()

    @pl.when(i == 0)
    def _():
        gather(tok_ref, 0, True)

    @pl.when(i >= 2)
    def _():
        scatter(slot, False)

    gather(tok_ref, slot, False)
    gather(tokn_ref, 1 - slot, True)

    xb = jnp.concatenate([xbuf[slot, pl.ds(c, bm, stride=8), :] for c in range(D_MODEL // 128)], axis=1).astype(BF16)
    hg = _dot(xb, wg_ref[0])
    hu = _dot(xb, wu_ref[0])
    h = (hg * (1.0 / (1.0 + jnp.exp(-hg))) * hu).astype(BF16)
    y = _dot(h, wd_ref[0]) * gate_ref[...]
    for c in range(D_MODEL // 128):
        ybuf[slot, pl.ds(c, bm, stride=8), :] = y[:, c * 128:(c + 1) * 128]
    scatter(slot, True)

    @pl.when(i == nb - 1)
    def _():
        scatter(slot, False)
        gather(tok_ref, 1 - slot, False)

        @pl.when(nb >= 2)
        def _():
            scatter(1 - slot, False)


def _moe(block_expert, slot_tok, slot_dst, slot_gate, x1, wg, wu, wd, n_rows_out):
    nb = slot_tok.shape[0]
    bm = MOE_BM
    grid_spec = pltpu.PrefetchScalarGridSpec(
        num_scalar_prefetch=1,
        grid=(nb,),
        in_specs=[pl.BlockSpec((1, 1, bm), lambda i, be: (i, 0, 0), memory_space=pltpu.SMEM),
                  pl.BlockSpec((1, 1, bm), lambda i, be: (jnp.minimum(i + 1, nb - 1), 0, 0),
                               memory_space=pltpu.SMEM),
                  pl.BlockSpec((1, 1, bm), lambda i, be: (i, 0, 0), memory_space=pltpu.SMEM),
                  pl.BlockSpec((bm, 1), lambda i, be: (i, 0)),
                  pl.BlockSpec(memory_space=pl.ANY),
                  pl.BlockSpec((1, D_MODEL, D_EXPERT), lambda i, be: (be[i], 0, 0)),
                  pl.BlockSpec((1, D_MODEL, D_EXPERT), lambda i, be: (be[i], 0, 0)),
                  pl.BlockSpec((1, D_EXPERT, D_MODEL), lambda i, be: (be[i], 0, 0))],
        out_specs=pl.BlockSpec(memory_space=pl.ANY),
        scratch_shapes=[pltpu.VMEM((2, bm * 8, 128), F32), pltpu.VMEM((2, bm * 8, 128), F32),
                        pltpu.SemaphoreType.DMA((2,)), pltpu.SemaphoreType.DMA((2,))],
    )
    return pl.pallas_call(
        _moe_kernel,
        grid_spec=grid_spec,
        out_shape=jax.ShapeDtypeStruct((n_rows_out * 8, 128), F32),
        compiler_params=_cparams(("arbitrary",)),
        name="moe_ffn",
    )(block_expert, slot_tok, slot_tok, slot_dst, slot_gate, x1, wg, wu, wd)


def _route_slots(eid, gate, hist, n_tok):
    bm = MOE_BM
    n_assign = 2 * n_tok
    flat_e = eid[0:2].reshape(-1)
    flat_g = gate[0:2].reshape(-1)
    order = jnp.argsort(flat_e).astype(jnp.int32)
    counts = jnp.sum(hist.reshape(-1, N_EXPERTS, 128)[:, :, 0], axis=0)
    seg_end = jnp.cumsum(counts)
    seg_start = seg_end - counts
    padded = (counts + bm - 1) // bm * bm
    pad_end = jnp.cumsum(padded)
    pad_start = pad_end - padded
    nb = n_assign // bm + N_EXPERTS
    n_slots = nb * bm
    block_start = jnp.arange(nb, dtype=jnp.int32) * bm
    block_expert = jnp.minimum(jnp.sum(block_start[:, None] >= pad_end[None, :], axis=1), N_EXPERTS - 1
                               ).astype(jnp.int32)
    slot = block_start[:, None] + jnp.arange(bm, dtype=jnp.int32)[None, :]
    rank = slot - pad_start[block_expert][:, None]
    valid = rank < counts[block_expert][:, None]
    src = order[jnp.where(valid, seg_start[block_expert][:, None] + rank, 0)]
    slot_tok = jnp.where(valid, src % n_tok, 0)
    slot_dst = jnp.where(valid, src, n_assign + slot - seg_end[block_expert][:, None])
    slot_gate = jnp.where(valid, flat_g[src], 0.0)
    return (block_expert, slot_tok.reshape(nb, 1, bm), slot_dst.reshape(nb, 1, bm), slot_gate.reshape(n_slots, 1))


def _final_kernel(x1_ref, y0_ref, y1_ref, p_ref, wpg_ref, wpp_ref, g2_ref, b2_ref, o_ref, *, alpha):
    x1 = x1_ref[...]
    gate = 1.0 / (1.0 + jnp.exp(-_dot(x1.astype(BF16), wpg_ref[...])))
    ple = gate * _dot(p_ref[...].astype(BF16), wpp_ref[...])
    tm = x1.shape[0]
    ffn = jnp.concatenate([y0_ref[pl.ds(c, tm, stride=8), :] + y1_ref[pl.ds(c, tm, stride=8), :]
                           for c in range(D_MODEL // 128)], axis=1)
    o_ref[...] = _layer_norm(alpha * x1 + ffn + ple, g2_ref[...], b2_ref[...])


def _final(x1, y2, p, wpg, wpp, g2, b2, alpha):
    n = x1.shape[0]
    tm = ROW_TILE
    nt = n // tm
    row = lambda w: pl.BlockSpec((tm, w), lambda i: (i, 0))
    return pl.pallas_call(
        functools.partial(_final_kernel, alpha=alpha),
        grid=(nt,),
        in_specs=[row(D_MODEL), pl.BlockSpec((tm * 8, 128), lambda i: (i, 0)),
                  pl.BlockSpec((tm * 8, 128), lambda i: (i + nt, 0)), row(PLE_DIM),
                  _full(wpg.shape), _full(wpp.shape), _full(g2.shape), _full(b2.shape)],
        out_specs=row(D_MODEL),
        out_shape=jax.ShapeDtypeStruct((n, D_MODEL), F32),
        compiler_params=_cparams(("parallel",)),
        name="final",
    )(x1, y2, y2, p, wpg, wpp, g2, b2)


def _rope_tables(seq_len):
    rows = seq_len // GRID_W
    row = jnp.repeat(jnp.arange(rows, dtype=F32), GRID_W)
    col = jnp.tile(jnp.arange(GRID_W, dtype=F32), rows)
    n_pairs = HEAD_DIM // 4
    inv_freq = ROPE_THETA ** (-jnp.arange(n_pairs, dtype=F32) / n_pairs)
    ang = jnp.concatenate([row[:, None] * inv_freq, col[:, None] * inv_freq], axis=-1)
    cos = jnp.repeat(jnp.cos(ang), 2, axis=-1)
    sin = jnp.repeat(jnp.sin(ang), 2, axis=-1) * jnp.tile(jnp.asarray([-1.0, 1.0], F32), HEAD_DIM // 2)
    return jnp.tile(cos, (1, 2)), jnp.tile(sin, (1, 2))


def _pad_cols(w, width):
    return jnp.pad(w, ((0, 0), (0, width - w.shape[1])))


def kernel(x_prompt, x_sample, p_prompt, p_sample, w_in, a_q_norm, a_k_norm, b_lambda, b_subln, c_gate_w2,
           c_gate_b, c_norm, w_out, ln1_g, ln1_b, w_router_coarse, b_router_coarse, w_router_fine, b_router_fine,
           w_exp_gate, w_exp_up, w_exp_down, w_ple_gate, w_ple_proj, ln2_g, ln2_b):
    depth = w_in.shape[0]
    alpha = (2.0 * depth) ** 0.25
    bp, tp, _ = x_prompt.shape
    bs, ts, _ = x_sample.shape
    n_p, n_s = bp * tp, bs * ts
    n_tok = n_p + n_s
    tm = ROW_TILE
    assert tp % tm == 0 and ts % tm == 0 and tp % GRID_W == 0 and ts % GRID_W == 0
    assert (2 * n_tok) % MOE_BM == 0

    cos_t, sin_t = _rope_tables(max(tp, ts))
    np_tiles, tp_tiles, ts_tiles = n_p // tm, tp // tm, ts // tm

    def pos_block(i):
        return jnp.where(i < np_tiles, i % tp_tiles, (i - np_tiles) % ts_tiles)

    e64 = _block_diag_avg(A_QW, HEAD_DIM)
    e96 = _block_diag_avg(C_VW, C_V_DIM)
    e32 = jnp.asarray(np.arange(B_QW)[:, None] // B_QK_DIM == np.arange(128)[None, :], BF16)

    def tpos(ks):
        li = (np.arange(ATT_COL)[None, :] - np.arange(ks)[:, None]).astype(np.float32)
        return jnp.stack([jnp.asarray(li) * jnp.float32(_alibi_slope(h) * LOG2E) for h in range(B_HEADS)])
    x = jnp.concatenate([x_prompt.reshape(n_p, D_MODEL), x_sample.reshape(n_s, D_MODEL)], axis=0)

    def per_group(t, width):
        return t[:n_p].reshape(bp, tp, width), t[n_p:].reshape(bs, ts, width)

    def group_views(t, width):
        def view(seq, first_row, bsz):
            if n_tok % seq == 0 and first_row % seq == 0:
                return t.reshape(n_tok // seq, seq, width), first_row // seq
            return t[first_row:first_row + bsz * seq].reshape(bsz, seq, width), 0
        return view(tp, 0, bp), view(ts, n_p, bs)

    def transposed(t, width):
        a, b = per_group(t, width)
        return a.transpose(0, 2, 1), b.transpose(0, 2, 1)

    def key_tiled_t(t, n_heads, ks_max):
        def one(v, bsz, seq):
            ks = min(ks_max, seq)
            vt = v.reshape(bsz, seq // ks, ks, n_heads, HEAD_DIM).transpose(0, 1, 3, 4, 2)
            ones = jnp.ones((bsz, seq // ks, n_heads, ATT_VROWS - HEAD_DIM, ks), BF16)
            return jnp.concatenate([vt, ones], axis=3).reshape(bsz, seq // ks, n_heads * ATT_VROWS, ks), ks
        a, b = per_group(t, n_heads * HEAD_DIM)
        return one(a, bp, tp), one(b, bs, ts)

    def merge_t(a, b, width):
        return jnp.concatenate([a.transpose(0, 2, 1).reshape(n_p, width),
                                b.transpose(0, 2, 1).reshape(n_s, width)], axis=0)

    def merge(a, b, width):
        return jnp.concatenate([a.reshape(n_p, width), b.reshape(n_s, width)], axis=0)

    for i in range(depth):
        offs = np.cumsum([0, A_QW, A_KW, A_KW, B_QW, B_QW, B_VW, C_KW, C_KW, C_VW, C_GATE_RANK, C_GATE_RANK, C_VW])
        cols = [w_in[i][:, offs[j]:offs[j + 1]] for j in range(12)]
        wa = jnp.concatenate(cols[0:3], axis=1).astype(BF16)
        wb = jnp.concatenate(cols[3:6], axis=1).astype(BF16)
        wc = jnp.concatenate([_pad_cols(cols[6], C_KP), _pad_cols(cols[7], C_KP), cols[8], cols[11],
                              _pad_cols(jnp.concatenate([cols[9], cols[10]], axis=1), 128)], axis=1).astype(BF16)
        w2 = c_gate_w2[i]
        w2f = jnp.zeros((128, C_KP), F32).at[:C_GATE_RANK, :C_KW].set(w2[0]).astype(BF16)
        w2b = jnp.zeros((128, C_KP), F32).at[C_GATE_RANK:2 * C_GATE_RANK, :C_KW].set(w2[1]).astype(BF16)
        gbf = _pad_cols(c_gate_b[i, 0][None, :], C_KP)
        gbb = _pad_cols(c_gate_b[i, 1][None, :], C_KP)
        gq = jnp.tile(a_q_norm[i], A_HEADS)[None, :]
        gk = jnp.tile(a_k_norm[i], A_KV_HEADS)[None, :]

        (qa, ka, va, qb, kb, vb, cq, ck, cv, gf, gb, og, bn) = _inproj(
            x, wa, wb, wc, w2f, w2b, gbf, gbb, gq, gk, e64, e32, cos_t, sin_t, pos_block)

        qa_p, qa_s = transposed(qa, A_QW)
        (ka_p, kp0), (ka_s, ks0) = group_views(ka, A_KW)
        (va_p, ks_p), (va_s, ks_s) = key_tiled_t(va, A_KV_HEADS, ATT_KS_A)
        oa = merge_t(_attn_a(qa_p, ka_p, kp0, va_p, ks_p), _attn_a(qa_s, ka_s, ks0, va_s, ks_s), A_QW)

        lam_init = 0.8 - 0.6 * math.exp(-0.3 * i)
        lv = b_lambda[i].astype(F32)
        lam = (jnp.exp(jnp.sum(lv[0] * lv[1])) - jnp.exp(jnp.sum(lv[2] * lv[3])) + lam_init).reshape(1, 1)
        gain_col = b_subln[i].reshape(B_V_DIM, 1)
        qb_p, qb_s = transposed(qb, B_QW)
        (kb_p, kp0), (kb_s, ks0) = group_views(kb, B_QW)
        (vb_p, ks_p), (vb_s, ks_s) = key_tiled_t(vb, B_HEADS, ATT_KS_B)
        bn_p, bn_s = bn[:np_tiles * 8], bn[np_tiles * 8:]
        ob = merge_t(_attn_b(_alibi_window(bn_p, bp, tp, min(ATT_TQ, tp), ks_p), lam, qb_p, kb_p, kp0, vb_p, tpos(ks_p), gain_col,
                             1.0 - lam_init, ks_p),
                     _attn_b(_alibi_window(bn_s, bs, ts, min(ATT_TQ, ts), ks_s), lam, qb_s, kb_s, ks0, vb_s, tpos(ks_s), gain_col,
                             1.0 - lam_init, ks_s), B_VW)

        (cq_p, cp0), (cq_s, cs0) = group_views(cq, C_KP)
        (ck_p, _), (ck_s, _) = group_views(ck, C_KP)
        (cv_p, _), (cv_s, _) = group_views(cv, C_VW)
        (gf_p, _), (gf_s, _) = group_views(gf, C_KP)
        (gb_p, _), (gb_s, _) = group_views(gb, C_KP)
        ocf_p, ocb_p = _gla(cq_p, ck_p, cv_p, gf_p, gb_p, bp, cp0)
        ocf_s, ocb_s = _gla(cq_s, ck_s, cv_s, gf_s, gb_s, bs, cs0)
        ocf = merge(ocf_p, ocf_s, C_VW)
        ocb = merge(ocb_p, ocb_s, C_VW)

        wo = w_out[i].astype(BF16)
        wr = jnp.zeros((ROUTE_ROWS, D_MODEL), F32)
        wr = wr.at[0:N_GROUPS].set(w_router_coarse[i].T).at[8:].set(w_router_fine[i].T)
        wrh = wr.astype(BF16)
        wrl = (wr - wrh.astype(F32)).astype(BF16)
        br = jnp.zeros((ROUTE_ROWS, 1), F32)
        br = br.at[0:N_GROUPS, 0].set(b_router_coarse[i].astype(F32)).at[8:, 0].set(b_router_fine[i].astype(F32))
        x1, x1t, eid, gate, hist = _outproj(
            x, oa, ob, ocf, ocb, og, jnp.tile(c_norm[i], C_HEADS)[None, :], e96,
            wo[:A_QW], wo[A_QW:A_QW + B_VW], wo[A_QW + B_VW:], ln1_g[i][None, :], ln1_b[i][None, :],
            wrh, wrl, br, alpha)

        block_expert, slot_tok, slot_dst, slot_gate = _route_slots(eid, gate, hist, n_tok)
        y2 = _moe(block_expert, slot_tok, slot_dst, slot_gate, x1t, w_exp_gate[i].astype(BF16),
                  w_exp_up[i].astype(BF16), w_exp_down[i].astype(BF16), slot_gate.shape[0])

        p_all = jnp.concatenate([p_prompt[i].reshape(n_p, PLE_DIM), p_sample[i].reshape(n_s, PLE_DIM)], axis=0)
        x = _final(x1, y2, p_all, w_ple_gate[i].astype(BF16), w_ple_proj[i].astype(BF16),
                   ln2_g[i][None, :], ln2_b[i][None, :], alpha)

    return x[:n_p].reshape(bp, tp, D_MODEL), x[n_p:].reshape(bs, ts, D_MODEL)
```

```python
import functools
import math

import numpy as np
import jax
import jax.numpy as jnp
from jax import lax
from jax.experimental import pallas as pl
from jax.experimental.pallas import tpu as pltpu

F32 = jnp.float32
BF16 = jnp.bfloat16

D_MODEL = 1024
GRID_W = 64
HEAD_DIM = 64
NORM_EPS = 1e-6
A_HEADS = 6
A_KV_HEADS = 2
A_GROUP = A_HEADS // A_KV_HEADS
ROPE_THETA = 10000.0
B_HEADS = 4
B_QK_DIM = 32
B_V_DIM = 64
B_MAPS = 2 * B_HEADS
C_HEADS = 4
C_K_DIM = 48
C_V_DIM = 96
C_GATE_RANK = 16
C_GATE_TAU = 16.0
C_CHUNK = 64
C_KW = C_HEADS * C_K_DIM
C_KP = 256
C_VW = C_HEADS * C_V_DIM
N_GROUPS = 4
EXPERTS_PER_GROUP = 8
N_EXPERTS = N_GROUPS * EXPERTS_PER_GROUP
D_EXPERT = 512
PLE_DIM = 256
A_QW = A_HEADS * HEAD_DIM
A_KW = A_KV_HEADS * HEAD_DIM
B_QW = B_HEADS * 2 * B_QK_DIM
B_VW = B_HEADS * B_V_DIM
ROUTE_ROWS = 8 + N_EXPERTS

VMEM_LIMIT = 56 * 1024 * 1024

ROW_TILE = 512
ATT_TQ = 1024
ATT_COL = 256
ATT_KS_A = 512
ATT_KS_B = 512
ATT_VROWS = HEAD_DIM + 16
LOG2E = math.log2(math.e)
GLA_BLOCK = 512
GLA_PAR = 2
MOE_BM = 256


def _cparams(sem):
    return pltpu.CompilerParams(dimension_semantics=sem, vmem_limit_bytes=VMEM_LIMIT)


def _full(shape):
    nd = len(shape)
    return pl.BlockSpec(shape, lambda *_: (0,) * nd)


def _dot(a, b):
    return jnp.dot(a, b, preferred_element_type=F32)


def _dot_nt(a, b):
    return lax.dot_general(a, b, (((1,), (1,)), ((), ())), preferred_element_type=F32)


def _dot_tn(a, b):
    return lax.dot_general(a, b, (((0,), (0,)), ((), ())), preferred_element_type=F32)


def _split2(x):
    hi = x.astype(BF16)
    lo = (x - hi.astype(F32)).astype(BF16)
    return hi, lo


def _split3(x):
    hi = x.astype(BF16)
    r = x - hi.astype(F32)
    mid = r.astype(BF16)
    lo = (r - mid.astype(F32)).astype(BF16)
    return hi, mid, lo


def _seg_mean(sq, e):
    hi, lo = _split2(sq)
    return _dot(hi, e) + _dot(lo, e)


def _block_diag_avg(width, seg):
    idx = np.arange(width) // seg
    return jnp.asarray((idx[:, None] == idx[None, :]).astype(np.float32) / seg, dtype=BF16)


def _rope(x, cos, sin_signed):
    pieces = []
    for c in range(x.shape[1] // 128):
        xc = x[:, c * 128:(c + 1) * 128]
        nxt = pltpu.roll(xc, 127, 1)
        prv = pltpu.roll(xc, 1, 1)
        lane = lax.broadcasted_iota(jnp.int32, xc.shape, 1)
        sw = jnp.where(lane % 2 == 0, nxt, prv)
        pieces.append(xc * cos + sw * sin_signed)
    return pieces[0] if len(pieces) == 1 else jnp.concatenate(pieces, axis=1)


def _store_vt(vt_ref, v, n_heads):
    vt = v.T
    for h in range(n_heads):
        vt_ref[0, h * ATT_VROWS:h * ATT_VROWS + HEAD_DIM, :] = vt[h * HEAD_DIM:(h + 1) * HEAD_DIM, :].astype(BF16)
        vt_ref[0, h * ATT_VROWS + HEAD_DIM:(h + 1) * ATT_VROWS, :] = jnp.ones(
            (ATT_VROWS - HEAD_DIM, vt.shape[1]), BF16)


def _inproj_kernel(x_ref, wa_ref, wb_ref, wc_ref, w2f_ref, w2b_ref, gbf_ref, gbb_ref, gq_ref, gk_ref,
                   e64_ref, e32_ref, cos_ref, sin_ref,
                   qat_ref, ka_ref, vat_ref, qbt_ref, kb_ref, vbt_ref, cq_ref, ck_ref, cv_ref, gf_ref, gb_ref,
                   og_ref, bn_ref):
    xb = x_ref[...].astype(BF16)
    cos = cos_ref[...]
    sin = sin_ref[...]

    za = _dot(xb, wa_ref[...])
    q = za[:, :A_QW]
    k = za[:, A_QW:A_QW + A_KW]
    e64 = e64_ref[...]
    qn = q * lax.rsqrt(_seg_mean(q * q, e64) + NORM_EPS) * gq_ref[...]
    kn = k * lax.rsqrt(_seg_mean(k * k, e64[:A_KW, :A_KW]) + NORM_EPS) * gk_ref[...]
    qat_ref[...] = (_rope(qn, cos, sin) * (HEAD_DIM ** -0.5 * LOG2E)).T.astype(BF16)
    ka_ref[...] = _rope(kn, cos, sin).astype(BF16)
    _store_vt(vat_ref, za[:, A_QW + A_KW:], A_KV_HEADS)

    zb = _dot(xb, wb_ref[...])
    qb_f = zb[:, :B_QW] * (B_QK_DIM ** -0.5 * LOG2E)
    qb = qb_f.astype(BF16)
    kb = zb[:, B_QW:2 * B_QW].astype(BF16)
    qbt_ref[...] = qb_f.T.astype(BF16)
    kb_ref[...] = kb
    _store_vt(vbt_ref, zb[:, 2 * B_QW:], B_HEADS)

    def max_sqnorm(t):
        tf = t.astype(F32)
        hi, lo = _split2(tf * tf)
        return jnp.max(_dot(hi, e32_ref[...]) + _dot(lo, e32_ref[...]), axis=0, keepdims=True)

    r8 = lax.broadcasted_iota(jnp.int32, (8, 128), 0)
    bn_ref[...] = jnp.where(r8 == 0, max_sqnorm(qb), jnp.where(r8 == 1, max_sqnorm(kb), 0.0))

    zc = _dot(xb, wc_ref[...])
    cq_ref[...] = zc[:, :C_KP] * (C_K_DIM ** -0.5)
    ck_ref[...] = zc[:, C_KP:2 * C_KP]
    cv_ref[...] = zc[:, 2 * C_KP:2 * C_KP + C_VW]
    og = zc[:, 2 * C_KP + C_VW:2 * C_KP + 2 * C_VW]
    og_ref[...] = og * (1.0 / (1.0 + jnp.exp(-og)))
    lr = zc[:, 2 * C_KP + 2 * C_VW:].astype(BF16)

    def log_decay(w2_ref, bias_ref):
        g = _dot(lr, w2_ref[...]) + bias_ref[...]
        return (jnp.minimum(g, 0.0) - jnp.log(1.0 + jnp.exp(-jnp.abs(g)))) * (1.0 / C_GATE_TAU)

    gf_ref[...] = log_decay(w2f_ref, gbf_ref)
    gb_ref[...] = log_decay(w2b_ref, gbb_ref)


def _inproj(x, wa, wb, wc, w2f, w2b, gbf, gbb, gq, gk, e64, e32, cos_t, sin_t, pos_block):
    n = x.shape[0]
    tm = ROW_TILE
    row = lambda w: pl.BlockSpec((tm, w), lambda i: (i, 0))
    tab = pl.BlockSpec((tm, 128), lambda i: (pos_block(i), 0))
    colt = lambda w: (pl.BlockSpec((w, tm), lambda i: (0, i)), jax.ShapeDtypeStruct((w, n), BF16))
    vtile = lambda h: (pl.BlockSpec((1, h * ATT_VROWS, tm), lambda i: (i, 0, 0)),
                       jax.ShapeDtypeStruct((n // tm, h * ATT_VROWS, tm), BF16))
    rowo = lambda w, dt: (row(w), jax.ShapeDtypeStruct((n, w), dt))
    outs = [colt(A_QW), rowo(A_KW, BF16), vtile(A_KV_HEADS), colt(B_QW), rowo(B_QW, BF16), vtile(B_HEADS),
            rowo(C_KP, F32), rowo(C_KP, F32), rowo(C_VW, F32), rowo(C_KP, F32), rowo(C_KP, F32), rowo(C_VW, F32),
            (pl.BlockSpec((8, 128), lambda i: (i, 0)), jax.ShapeDtypeStruct((n // tm * 8, 128), F32))]
    return pl.pallas_call(
        _inproj_kernel,
        grid=(n // tm,),
        in_specs=[row(D_MODEL), _full(wa.shape), _full(wb.shape), _full(wc.shape), _full(w2f.shape),
                  _full(w2b.shape), _full(gbf.shape), _full(gbb.shape), _full(gq.shape), _full(gk.shape),
                  _full(e64.shape), _full(e32.shape), tab, tab],
        out_specs=[s for s, _ in outs],
        out_shape=[t for _, t in outs],
        compiler_params=_cparams(("parallel",)),
        name="inproj",
    )(x, wa, wb, wc, w2f, w2b, gbf, gbb, gq, gk, e64, e32, cos_t, sin_t)


def _attn_loop(k_ref, vt_ref, qpad, s_scr, m_ref, acc_ref, ks, vhead_of_map):
    n_maps, _, tq = qpad.shape
    ncol = tq // ATT_COL
    n_sub = k_ref.shape[1] // ks
    m_ref[...] = jnp.full(m_ref.shape, -jnp.inf, F32)
    acc_ref[...] = jnp.zeros(acc_ref.shape, F32)

    k0 = k_ref[0, 0:ks, :]
    for mp in range(n_maps):
        for c in range(ncol):
            s_scr[mp, c] = _dot(k0, qpad[mp, :, c * ATT_COL:(c + 1) * ATT_COL])

    def key_tile(j, carry):
        jn = jnp.minimum(j + 1, n_sub - 1)
        kn = k_ref[0, pl.ds(pl.multiple_of(jn * ks, ks), ks), :]
        for c in range(ncol):
            cs = slice(c * ATT_COL, (c + 1) * ATT_COL)
            for mp in range(n_maps):
                vh = vhead_of_map[mp]
                s = s_scr[mp, c]
                s_scr[mp, c] = _dot(kn, qpad[mp, :, cs])
                m_prev = m_ref[mp, :, cs]
                m_new = jnp.maximum(m_prev, jnp.max(s, axis=0, keepdims=True))
                p = jnp.exp2(s - m_new).astype(BF16)
                vt = vt_ref[j, vh * ATT_VROWS:(vh + 1) * ATT_VROWS, :]
                acc_ref[mp, :, cs] = jnp.exp2(m_prev - m_new) * acc_ref[mp, :, cs] + _dot(vt, p)
                m_ref[mp, :, cs] = m_new
        return carry

    lax.fori_loop(0, n_sub, key_tile, 0)


def _attn_out(acc_ref, mp):
    return acc_ref[mp, 0:HEAD_DIM, :] / acc_ref[mp, HEAD_DIM:HEAD_DIM + 1, :]


def _attn_a_kernel(qt_ref, k_ref, vt_ref, ot_ref, qpad, s_scr, m_ref, acc_ref, *, ks):
    qpad[...] = jnp.zeros(qpad.shape, BF16)
    for h in range(A_HEADS):
        g = h // A_GROUP
        qpad[h, g * HEAD_DIM:(g + 1) * HEAD_DIM, :] = qt_ref[h * HEAD_DIM:(h + 1) * HEAD_DIM, :]
    _attn_loop(k_ref, vt_ref, qpad, s_scr, m_ref, acc_ref, ks, tuple(h // A_GROUP for h in range(A_HEADS)))
    for h in range(A_HEADS):
        ot_ref[0, h * HEAD_DIM:(h + 1) * HEAD_DIM, :] = _attn_out(acc_ref, h).astype(BF16)


def _attn_scratch(n_maps, qk_width, tq, ks):
    return [pltpu.VMEM((n_maps, qk_width, tq), BF16),
            pltpu.VMEM((n_maps, tq // ATT_COL, ks, ATT_COL), F32),
            pltpu.VMEM((n_maps, 1, tq), F32),
            pltpu.VMEM((n_maps, ATT_VROWS, tq), F32)]


def _attn_a(qt, k, vt, bsz, b0, ks):
    seq = k.shape[1]
    tq = min(ATT_TQ, seq)
    nq = seq // tq
    return pl.pallas_call(
        functools.partial(_attn_a_kernel, ks=ks),
        grid=(bsz, nq),
        in_specs=[pl.BlockSpec((A_QW, tq), lambda b, qi: (0, (b + b0) * nq + qi)),
                  pl.BlockSpec((1, seq, A_KW), lambda b, qi: (b + b0, 0, 0), pipeline_mode=pl.Buffered(1)),
                  pl.BlockSpec((seq // ks, A_KV_HEADS * ATT_VROWS, ks), lambda b, qi: (b + b0, 0, 0),
                               pipeline_mode=pl.Buffered(1))],
        out_specs=pl.BlockSpec((1, A_QW, tq), lambda b, qi: (b, 0, qi)),
        out_shape=jax.ShapeDtypeStruct((bsz, A_QW, seq), BF16),
        scratch_shapes=_attn_scratch(A_HEADS, A_KW, tq, ks),
        compiler_params=_cparams(("parallel", "arbitrary")),
        name="attn_a",
    )(qt, k, vt)


def _alibi_slope(h):
    return 2.0 ** (-8.0 * (h + 1) / B_HEADS)


def _attn_b_kernel(bnd_ref, lam_ref, qt_ref, k_ref, vt_ref, tpos_ref, gain_ref, ot_ref, qpad, s_scr, m_ref,
                   acc_ref, *, ks, out_scale):
    b = pl.program_id(0)
    qi = pl.program_id(1)
    tq = qt_ref.shape[1]
    ncol = tq // ATT_COL
    n_sub = k_ref.shape[1] // ks
    n_diag = tq // ks
    d_lo = qi * n_diag

    qpad[...] = jnp.zeros(qpad.shape, BF16)
    for mp in range(B_MAPS):
        qpad[mp, mp * B_QK_DIM:(mp + 1) * B_QK_DIM, :] = qt_ref[mp * B_QK_DIM:(mp + 1) * B_QK_DIM, :]
    m_ref[...] = jnp.full(m_ref.shape, -jnp.inf, F32)
    acc_ref[...] = jnp.zeros(acc_ref.shape, F32)

    def key_rows(j):
        return k_ref[0, pl.ds(pl.multiple_of(j * ks, ks), ks), :]

    for h in range(B_HEADS):
        slope = _alibi_slope(h) * LOG2E
        base = ((b * pl.num_programs(1) + qi) * B_HEADS + h) * 2
        j_lo = bnd_ref[base]
        j_hi = bnd_ref[base + 1]

        k0 = key_rows(j_lo)
        for mi in range(2):
            for c in range(ncol):
                s_scr[mi, c] = _dot(k0, qpad[2 * h + mi, :, c * ATT_COL:(c + 1) * ATT_COL])

        def key_tile(j, carry, side, h=h, slope=slope):
            kn = key_rows(jnp.minimum(j + 1, n_sub - 1))
            vt = vt_ref[j, h * ATT_VROWS:(h + 1) * ATT_VROWS, :]
            for c in range(ncol):
                cs = slice(c * ATT_COL, (c + 1) * ATT_COL)
                delta = qi * tq + c * ATT_COL - j * ks
                if side == 0:
                    rel = (lax.broadcasted_iota(jnp.int32, (ks, ATT_COL), 1)
                           - lax.broadcasted_iota(jnp.int32, (ks, ATT_COL), 0) + delta)
                    bias = jnp.abs(rel).astype(F32) * (-slope)
                    kappa = 0.0
                else:
                    kappa = delta.astype(F32) * (-side * slope)
                for mi in range(2):
                    mp = 2 * h + mi
                    if side == 0:
                        s = s_scr[mi, c] + bias
                    elif side == 1:
                        s = s_scr[mi, c] - tpos_ref[h]
                    else:
                        s = s_scr[mi, c] + tpos_ref[h]
                    s_scr[mi, c] = _dot(kn, qpad[mp, :, cs])
                    m_prev = m_ref[mp, :, cs]
                    m_new = jnp.maximum(m_prev, jnp.max(s, axis=0, keepdims=True) + kappa)
                    p = jnp.exp2(s - (m_new - kappa)).astype(BF16)
                    acc_ref[mp, :, cs] = jnp.exp2(m_prev - m_new) * acc_ref[mp, :, cs] + _dot(vt, p)
                    m_ref[mp, :, cs] = m_new
            return carry

        lax.fori_loop(j_lo, d_lo, functools.partial(key_tile, side=1), 0)
        for dj in range(n_diag):
            key_tile(d_lo + dj, 0, 0)
        lax.fori_loop(d_lo + n_diag, j_hi, functools.partial(key_tile, side=-1), 0)

    lam = lam_ref[0, 0]
    for h in range(B_HEADS):
        o = _attn_out(acc_ref, 2 * h) - lam * _attn_out(acc_ref, 2 * h + 1)
        ms = jnp.mean(o * o, axis=0, keepdims=True)
        o = o * lax.rsqrt(ms + NORM_EPS) * gain_ref[...] * out_scale
        ot_ref[0, h * B_V_DIM:(h + 1) * B_V_DIM, :] = o.astype(BF16)


def _alibi_window(bn_tiles, bsz, seq, tq, ks):
    nq = seq // tq
    n_sub = seq // ks
    t = bn_tiles.reshape(bsz, nq, -1, 8, 128)
    q2 = t[:, :, :, 0, :B_MAPS].max(axis=2).reshape(bsz, nq, B_HEADS, 2).max(axis=-1)
    k2 = t[:, :, :, 1, :B_MAPS].max(axis=(1, 2)).reshape(bsz, B_HEADS, 2).max(axis=-1)
    u = jnp.sqrt(q2 * k2[:, None, :]) * 1.01
    slope = jnp.asarray([_alibi_slope(h) * LOG2E for h in range(B_HEADS)], F32)
    reach = (2.0 * u + 152.0) / slope
    w = jnp.minimum(jnp.floor((reach + (ks - 1)) / ks), n_sub).astype(jnp.int32)
    d_lo = jnp.arange(nq, dtype=jnp.int32)[None, :, None] * (tq // ks)
    return jnp.stack([jnp.maximum(d_lo - w, 0), jnp.minimum(d_lo + tq // ks + w, n_sub)], axis=-1).reshape(-1)


def _attn_b(bounds, lam, qt, k, vt, tpos, gain_col, out_scale, bsz, b0, ks):
    seq = k.shape[1]
    tq = min(ATT_TQ, seq)
    nq = seq // tq
    assert tq % ks == 0
    grid_spec = pltpu.PrefetchScalarGridSpec(
        num_scalar_prefetch=1,
        grid=(bsz, nq),
        in_specs=[pl.BlockSpec((1, 1), lambda b, qi, bnd: (0, 0), memory_space=pltpu.SMEM),
                  pl.BlockSpec((B_QW, tq), lambda b, qi, bnd: (0, (b + b0) * nq + qi)),
                  pl.BlockSpec((1, seq, B_QW), lambda b, qi, bnd: (b + b0, 0, 0), pipeline_mode=pl.Buffered(1)),
                  pl.BlockSpec((seq // ks, B_HEADS * ATT_VROWS, ks), lambda b, qi, bnd: (b + b0, 0, 0),
                               pipeline_mode=pl.Buffered(1)),
                  pl.BlockSpec((B_HEADS, ks, ATT_COL), lambda b, qi, bnd: (0, 0, 0), pipeline_mode=pl.Buffered(1)),
                  pl.BlockSpec((B_V_DIM, 1), lambda b, qi, bnd: (0, 0))],
        out_specs=pl.BlockSpec((1, B_VW, tq), lambda b, qi, bnd: (b, 0, qi)),
        scratch_shapes=[pltpu.VMEM((B_MAPS, B_QW, tq), BF16),
                        pltpu.VMEM((2, tq // ATT_COL, ks, ATT_COL), F32),
                        pltpu.VMEM((B_MAPS, 1, tq), F32),
                        pltpu.VMEM((B_MAPS, ATT_VROWS, tq), F32)],
    )
    return pl.pallas_call(
        functools.partial(_attn_b_kernel, ks=ks, out_scale=out_scale),
        grid_spec=grid_spec,
        out_shape=jax.ShapeDtypeStruct((bsz, B_VW, seq), BF16),
        compiler_params=_cparams(("parallel", "arbitrary")),
        name="attn_b",
    )(bounds, lam, qt, k, vt, tpos, gain_col)


def _gla_masks():
    j = np.arange(C_CHUNK)
    tri_f = (j[None, :] <= j[:, None]).astype(np.float32)
    tri_b = (j[None, :] >= j[:, None]).astype(np.float32)
    d = np.arange(C_KP)
    dhead = np.where(d < C_KW, d // C_K_DIM, -1)
    hj = np.arange(C_HEADS * C_CHUNK) // C_CHUNK
    vhead = np.arange(C_VW) // C_V_DIM
    m_stack = (hj[:, None] == dhead[None, :]).astype(np.float32)
    m_v = (hj[:, None] == vhead[None, :]).astype(np.float32)
    m_s = (vhead[:, None] == dhead[None, :]).astype(np.float32)
    return (jnp.asarray(np.stack([tri_f, tri_b]), BF16),
            jnp.asarray(np.stack([np.tile(tri_f, (1, C_HEADS)), np.tile(tri_b, (1, C_HEADS))]), F32),
            jnp.asarray(m_stack, F32), jnp.asarray(m_v, F32), jnp.asarray(m_s, F32))


def _gla_kernel(qf_ref, kf_ref, vf_ref, gf_ref, qb_ref, kb_ref, vb_ref, gb_ref, tri_ref, mp_ref, mstack_ref,
                mv_ref, ms_ref, of_ref, ob_ref, st_ref):
    @pl.when(pl.program_id(1) == 0)
    def _():
        st_ref[...] = jnp.zeros(st_ref.shape, F32)

    nb = qf_ref.shape[0]
    n_chunks = qf_ref.shape[1] // C_CHUNK
    chains = [(d, e) for e in range(nb) for d in range(2)]
    srcs = ((qf_ref, kf_ref, vf_ref, gf_ref, of_ref), (qb_ref, kb_ref, vb_ref, gb_ref, ob_ref))

    def chunk(ci, carry):
        rows = [pl.ds(pl.multiple_of((ci if d == 0 else n_chunks - 1 - ci) * C_CHUNK, C_CHUNK), C_CHUNK)
                for d, _ in chains]
        bs = []
        for (d, e), r in zip(chains, rows):
            g_hi, g_mid, g_lo = _split3(srcs[d][3][e, r, :])
            tri = tri_ref[d]
            bs.append(_dot(tri, g_hi) + _dot(tri, g_mid) + _dot(tri, g_lo))
        aa, khats, ps, edges = [], [], [], []
        for (d, e), r, b in zip(chains, rows, bs):
            edge = C_CHUNK - 1 if d == 0 else 0
            b_edge = b[edge:edge + 1, :]
            q = srcs[d][0][e, r, :]
            k = srcs[d][1][e, r, :]
            a = (q * jnp.exp(b)).astype(BF16)
            bm = k * jnp.exp(-b)
            khats.append((k * jnp.exp(b_edge - b)).astype(BF16))
            bm_stack = (jnp.concatenate([bm] * C_HEADS, axis=0) * mstack_ref[...]).astype(BF16)
            ps.append((_dot_nt(a, bm_stack) * mp_ref[d]).astype(BF16))
            aa.append(a)
            edges.append(jnp.exp(b_edge))
        for ix, ((d, e), r) in enumerate(zip(chains, rows)):
            v = srcs[d][2][e, r, :]
            v_bd = (jnp.concatenate([v] * C_HEADS, axis=0) * mv_ref[...]).astype(BF16)
            st = st_ref[ix]
            srcs[d][4][e, r, :] = _dot(ps[ix], v_bd) + _dot_nt(aa[ix], st.astype(BF16))
            st_ref[ix] = st * edges[ix] + _dot_tn(v.astype(BF16), khats[ix]) * ms_ref[...]
        return carry

    lax.fori_loop(0, n_chunks, chunk, 0)


def _gla(q, k, v, gf, gb, bsz, b0):
    seq = q.shape[1]
    blk = min(GLA_BLOCK, seq)
    nblk = seq // blk
    nb = GLA_PAR if bsz % GLA_PAR == 0 and b0 % GLA_PAR == 0 else 1
    masks = _gla_masks()
    fmap = lambda b, t: (b + b0 // nb, t, 0)
    bmap = lambda b, t: (b + b0 // nb, nblk - 1 - t, 0)
    omap = lambda b, t: (b, t, 0)
    ormap = lambda b, t: (b, nblk - 1 - t, 0)
    kw = lambda m: pl.BlockSpec((nb, blk, C_KP), m)
    vw = lambda m: pl.BlockSpec((nb, blk, C_VW), m)
    return pl.pallas_call(
        _gla_kernel,
        grid=(bsz // nb, nblk),
        in_specs=[kw(fmap), kw(fmap), vw(fmap), kw(fmap), kw(bmap), kw(bmap), vw(bmap), kw(bmap)]
                 + [_full(m.shape) for m in masks],
        out_specs=[vw(omap), vw(ormap)],
        out_shape=[jax.ShapeDtypeStruct((bsz, seq, C_VW), F32)] * 2,
        scratch_shapes=[pltpu.VMEM((2 * nb, C_VW, C_KP), F32)],
        compiler_params=_cparams(("parallel", "arbitrary")),
        name="gla",
    )(q, k, v, gf, q, k, v, gb, *masks)


def _layer_norm(x, g, b):
    xc = x - jnp.mean(x, axis=-1, keepdims=True)
    var = jnp.mean(xc * xc, axis=-1, keepdims=True)
    return xc * lax.rsqrt(var + NORM_EPS) * g + b


def _outproj_kernel(x_ref, oa_ref, ob_ref, ocf_ref, ocb_ref, og_ref, cg_ref, e96_ref, woa_ref, wob_ref, woc_ref,
                    g1_ref, b1_ref, wrh_ref, wrl_ref, br_ref, x1_ref, x1t_ref, eid_ref, gate_ref, hist_ref, *, alpha):
    oc = ocf_ref[...] + ocb_ref[...]
    oc = oc * lax.rsqrt(_seg_mean(oc * oc, e96_ref[...]) + NORM_EPS) * cg_ref[...] * og_ref[...]
    mixed = (_dot(oa_ref[...], woa_ref[...]) + _dot(ob_ref[...], wob_ref[...])
             + _dot(oc.astype(BF16), woc_ref[...]))
    x1 = _layer_norm(alpha * x_ref[...] + mixed, g1_ref[...], b1_ref[...])
    x1_ref[...] = x1
    for c in range(D_MODEL // 128):
        x1t_ref[pl.ds(c, x1.shape[0], stride=8), :] = x1[:, c * 128:(c + 1) * 128]

    x_hi, x_lo = _split2(x1)
    wrh = wrh_ref[...]
    lt = _dot_nt(wrh, x_hi) + _dot_nt(wrh, x_lo) + _dot_nt(wrl_ref[...], x_hi) + br_ref[...]
    tm = lt.shape[1]
    coarse = lt[0:N_GROUPS]
    r4 = lax.broadcasted_iota(jnp.int32, (N_GROUPS, tm), 0)
    cmax = jnp.max(coarse, axis=0, keepdims=True)
    group = jnp.min(jnp.where(coarse == cmax, r4, N_GROUPS), axis=0, keepdims=True)
    group_w = 1.0 / jnp.sum(jnp.exp(coarse - cmax), axis=0, keepdims=True)
    fine = jnp.zeros((EXPERTS_PER_GROUP, tm), F32)
    for g in range(N_GROUPS):
        fine = jnp.where(group == g, lt[8 + 8 * g:16 + 8 * g], fine)
    r8 = lax.broadcasted_iota(jnp.int32, (EXPERTS_PER_GROUP, tm), 0)
    v1 = jnp.max(fine, axis=0, keepdims=True)
    i1 = jnp.min(jnp.where(fine == v1, r8, EXPERTS_PER_GROUP), axis=0, keepdims=True)
    rest = jnp.where(r8 == i1, -jnp.inf, fine)
    v2 = jnp.max(rest, axis=0, keepdims=True)
    i2 = jnp.min(jnp.where(rest == v2, r8, EXPERTS_PER_GROUP), axis=0, keepdims=True)
    e2 = jnp.exp(v2 - v1)
    w1 = group_w / (1.0 + e2)
    w2 = group_w * e2 / (1.0 + e2)
    e1 = group * EXPERTS_PER_GROUP + i1
    e2nd = group * EXPERTS_PER_GROUP + i2
    eid_ref[...] = jnp.where(r8 == 0, e1, jnp.where(r8 == 1, e2nd, 0))
    gate_ref[...] = jnp.where(r8 == 0, w1, jnp.where(r8 == 1, w2, 0.0))
    r32 = lax.broadcasted_iota(jnp.int32, (N_EXPERTS, tm), 0)
    hits = jnp.where(r32 == e1, 1.0, 0.0) + jnp.where(r32 == e2nd, 1.0, 0.0)
    hist_ref[...] = jnp.broadcast_to(jnp.sum(hits, axis=1, keepdims=True), (N_EXPERTS, 128)).astype(jnp.int32)


def _outproj(x, oa, ob, ocf, ocb, og, cg, e96, woa, wob, woc, g1, b1, wrh, wrl, br, alpha):
    n = x.shape[0]
    tm = ROW_TILE
    row = lambda w: pl.BlockSpec((tm, w), lambda i: (i, 0))
    col = pl.BlockSpec((8, tm), lambda i: (0, i))
    return pl.pallas_call(
        functools.partial(_outproj_kernel, alpha=alpha),
        grid=(n // tm,),
        in_specs=[row(D_MODEL), row(A_QW), row(B_VW), row(C_VW), row(C_VW), row(C_VW), _full(cg.shape),
                  _full(e96.shape), _full(woa.shape), _full(wob.shape), _full(woc.shape), _full(g1.shape),
                  _full(b1.shape), _full(wrh.shape), _full(wrl.shape), _full(br.shape)],
        out_specs=[row(D_MODEL), pl.BlockSpec((tm * 8, 128), lambda i: (i, 0)), col, col,
                   pl.BlockSpec((N_EXPERTS, 128), lambda i: (i, 0))],
        out_shape=[jax.ShapeDtypeStruct((n, D_MODEL), F32), jax.ShapeDtypeStruct((n * 8, 128), F32),
                   jax.ShapeDtypeStruct((8, n), jnp.int32),
                   jax.ShapeDtypeStruct((8, n), F32),
                   jax.ShapeDtypeStruct((n // tm * N_EXPERTS, 128), jnp.int32)],
        compiler_params=_cparams(("parallel",)),
        name="outproj",
    )(x, oa, ob, ocf, ocb, og, cg, e96, woa, wob, woc, g1, b1, wrh, wrl, br)


def _moe_kernel(be_ref, tok_ref, tokn_ref, dst_ref, gate_ref, x_hbm, wg_ref, wu_ref, wd_ref, y_hbm,
                xbuf, ybuf, gsem, ssem):
    i = pl.program_id(0)
    nb = pl.num_programs(0)
    bm = xbuf.shape[1] // 8
    slot = i % 2

    def token_rows(t):
        return pl.ds(pl.multiple_of(t * 8, 8), 8)

    def gather(idx_ref, sl, start):
        for r in range(bm):
            cp = pltpu.make_async_copy(x_hbm.at[token_rows(idx_ref[0, 0, r]), :], xbuf.at[sl, pl.ds(r * 8, 8), :],
                                       gsem.at[sl])
            if start:
                cp.start()
            else:
                cp.wait()

    def scatter(sl, start):
        for r in range(bm):
            cp = pltpu.make_async_copy(ybuf.at[sl, pl.ds(r * 8, 8), :], y_hbm.at[token_rows(dst_ref[0, 0, r]), :],
                                       ssem.at[sl])
            if start:
                cp.start()
            else:
                cp.wait()

    @pl.when(i == 0)
    def _():
        gather(tok_ref, 0, True)

    @pl.when(i >= 2)
    def _():
        scatter(slot, False)

    gather(tok_ref, slot, False)
    gather(tokn_ref, 1 - slot, True)

    xb = jnp.concatenate([xbuf[slot, pl.ds(c, bm, stride=8), :] for c in range(D_MODEL // 128)], axis=1).astype(BF16)
    hg = _dot(xb, wg_ref[0])
    hu = _dot(xb, wu_ref[0])
    h = (hg * (1.0 / (1.0 + jnp.exp(-hg))) * hu).astype(BF16)
    y = _dot(h, wd_ref[0]) * gate_ref[...]
    for c in range(D_MODEL // 128):
        ybuf[slot, pl.ds(c, bm, stride=8), :] = y[:, c * 128:(c + 1) * 128]
    scatter(slot, True)

    @pl.when(i == nb - 1)
    def _():
        scatter(slot, False)
        gather(tok_ref, 1 - slot, False)

        @pl.when(nb >= 2)
        def _():
            scatter(1 - slot, False)


def _moe(block_expert, slot_tok, slot_dst, slot_gate, x1, wg, wu, wd, n_rows_out):
    nb = slot_tok.shape[0]
    bm = MOE_BM
    grid_spec = pltpu.PrefetchScalarGridSpec(
        num_scalar_prefetch=1,
        grid=(nb,),
        in_specs=[pl.BlockSpec((1, 1, bm), lambda i, be: (i, 0, 0), memory_space=pltpu.SMEM),
                  pl.BlockSpec((1, 1, bm), lambda i, be: (jnp.minimum(i + 1, nb - 1), 0, 0),
                               memory_space=pltpu.SMEM),
                  pl.BlockSpec((1, 1, bm), lambda i, be: (i, 0, 0), memory_space=pltpu.SMEM),
                  pl.BlockSpec((bm, 1), lambda i, be: (i, 0)),
                  pl.BlockSpec(memory_space=pl.ANY),
                  pl.BlockSpec((1, D_MODEL, D_EXPERT), lambda i, be: (be[i], 0, 0)),
                  pl.BlockSpec((1, D_MODEL, D_EXPERT), lambda i, be: (be[i], 0, 0)),
                  pl.BlockSpec((1, D_EXPERT, D_MODEL), lambda i, be: (be[i], 0, 0))],
        out_specs=pl.BlockSpec(memory_space=pl.ANY),
        scratch_shapes=[pltpu.VMEM((2, bm * 8, 128), F32), pltpu.VMEM((2, bm * 8, 128), F32),
                        pltpu.SemaphoreType.DMA((2,)), pltpu.SemaphoreType.DMA((2,))],
    )
    return pl.pallas_call(
        _moe_kernel,
        grid_spec=grid_spec,
        out_shape=jax.ShapeDtypeStruct((n_rows_out * 8, 128), F32),
        compiler_params=_cparams(("arbitrary",)),
        name="moe_ffn",
    )(block_expert, slot_tok, slot_tok, slot_dst, slot_gate, x1, wg, wu, wd)


def _route_slots(eid, gate, hist, n_tok):
    bm = MOE_BM
    n_assign = 2 * n_tok
    flat_e = eid[0:2].reshape(-1)
    flat_g = gate[0:2].reshape(-1)
    order = jnp.argsort(flat_e).astype(jnp.int32)
    counts = jnp.sum(hist.reshape(-1, N_EXPERTS, 128)[:, :, 0], axis=0)
    seg_end = jnp.cumsum(counts)
    seg_start = seg_end - counts
    padded = (counts + bm - 1) // bm * bm
    pad_end = jnp.cumsum(padded)
    pad_start = pad_end - padded
    nb = n_assign // bm + N_EXPERTS
    n_slots = nb * bm
    block_start = jnp.arange(nb, dtype=jnp.int32) * bm
    block_expert = jnp.minimum(jnp.sum(block_start[:, None] >= pad_end[None, :], axis=1), N_EXPERTS - 1
                               ).astype(jnp.int32)
    slot = block_start[:, None] + jnp.arange(bm, dtype=jnp.int32)[None, :]
    rank = slot - pad_start[block_expert][:, None]
    valid = rank < counts[block_expert][:, None]
    src = order[jnp.where(valid, seg_start[block_expert][:, None] + rank, 0)]
    slot_tok = jnp.where(valid, src % n_tok, 0)
    slot_dst = jnp.where(valid, src, n_assign + slot - seg_end[block_expert][:, None])
    slot_gate = jnp.where(valid, flat_g[src], 0.0)
    return (block_expert, slot_tok.reshape(nb, 1, bm), slot_dst.reshape(nb, 1, bm), slot_gate.reshape(n_slots, 1))


def _final_kernel(x1_ref, y0_ref, y1_ref, p_ref, wpg_ref, wpp_ref, g2_ref, b2_ref, o_ref, *, alpha):
    x1 = x1_ref[...]
    gate = 1.0 / (1.0 + jnp.exp(-_dot(x1.astype(BF16), wpg_ref[...])))
    ple = gate * _dot(p_ref[...].astype(BF16), wpp_ref[...])
    tm = x1.shape[0]
    ffn = jnp.concatenate([y0_ref[pl.ds(c, tm, stride=8), :] + y1_ref[pl.ds(c, tm, stride=8), :]
                           for c in range(D_MODEL // 128)], axis=1)
    o_ref[...] = _layer_norm(alpha * x1 + ffn + ple, g2_ref[...], b2_ref[...])


def _final(x1, y2, p, wpg, wpp, g2, b2, alpha):
    n = x1.shape[0]
    tm = ROW_TILE
    nt = n // tm
    row = lambda w: pl.BlockSpec((tm, w), lambda i: (i, 0))
    return pl.pallas_call(
        functools.partial(_final_kernel, alpha=alpha),
        grid=(nt,),
        in_specs=[row(D_MODEL), pl.BlockSpec((tm * 8, 128), lambda i: (i, 0)),
                  pl.BlockSpec((tm * 8, 128), lambda i: (i + nt, 0)), row(PLE_DIM),
                  _full(wpg.shape), _full(wpp.shape), _full(g2.shape), _full(b2.shape)],
        out_specs=row(D_MODEL),
        out_shape=jax.ShapeDtypeStruct((n, D_MODEL), F32),
        compiler_params=_cparams(("parallel",)),
        name="final",
    )(x1, y2, y2, p, wpg, wpp, g2, b2)


def _rope_tables(seq_len):
    rows = seq_len // GRID_W
    row = jnp.repeat(jnp.arange(rows, dtype=F32), GRID_W)
    col = jnp.tile(jnp.arange(GRID_W, dtype=F32), rows)
    n_pairs = HEAD_DIM // 4
    inv_freq = ROPE_THETA ** (-jnp.arange(n_pairs, dtype=F32) / n_pairs)
    ang = jnp.concatenate([row[:, None] * inv_freq, col[:, None] * inv_freq], axis=-1)
    cos = jnp.repeat(jnp.cos(ang), 2, axis=-1)
    sin = jnp.repeat(jnp.sin(ang), 2, axis=-1) * jnp.tile(jnp.asarray([-1.0, 1.0], F32), HEAD_DIM // 2)
    return jnp.tile(cos, (1, 2)), jnp.tile(sin, (1, 2))


def _pad_cols(w, width):
    return jnp.pad(w, ((0, 0), (0, width - w.shape[1])))


def kernel(x_prompt, x_sample, p_prompt, p_sample, w_in, a_q_norm, a_k_norm, b_lambda, b_subln, c_gate_w2,
           c_gate_b, c_norm, w_out, ln1_g, ln1_b, w_router_coarse, b_router_coarse, w_router_fine, b_router_fine,
           w_exp_gate, w_exp_up, w_exp_down, w_ple_gate, w_ple_proj, ln2_g, ln2_b):
    depth = w_in.shape[0]
    alpha = (2.0 * depth) ** 0.25
    bp, tp, _ = x_prompt.shape
    bs, ts, _ = x_sample.shape
    n_p, n_s = bp * tp, bs * ts
    n_tok = n_p + n_s
    tm = ROW_TILE
    assert tp % tm == 0 and ts % tm == 0 and tp % GRID_W == 0 and ts % GRID_W == 0
    assert (2 * n_tok) % MOE_BM == 0
    assert n_tok % tp == 0 and n_tok % ts == 0 and n_p % ts == 0
    assert ATT_KS_A == tm and ATT_KS_B == tm

    cos_t, sin_t = _rope_tables(max(tp, ts))
    np_tiles, tp_tiles, ts_tiles = n_p // tm, tp // tm, ts // tm

    def pos_block(i):
        return jnp.where(i < np_tiles, i % tp_tiles, (i - np_tiles) % ts_tiles)

    e64 = _block_diag_avg(A_QW, HEAD_DIM)
    e96 = _block_diag_avg(C_VW, C_V_DIM)
    e32 = jnp.asarray(np.arange(B_QW)[:, None] // B_QK_DIM == np.arange(128)[None, :], BF16)

    def tpos(ks):
        li = (np.arange(ATT_COL)[None, :] - np.arange(ks)[:, None]).astype(np.float32)
        return jnp.stack([jnp.asarray(li) * jnp.float32(_alibi_slope(h) * LOG2E) for h in range(B_HEADS)])
    x = jnp.concatenate([x_prompt.reshape(n_p, D_MODEL), x_sample.reshape(n_s, D_MODEL)], axis=0)

    def group_views(t, width):
        return t.reshape(n_tok // tp, tp, width), t.reshape(n_tok // ts, ts, width)

    first_p, first_s = 0, n_p // ts

    def merge_t(a, b, width):
        return jnp.concatenate([a.transpose(0, 2, 1).reshape(n_p, width),
                                b.transpose(0, 2, 1).reshape(n_s, width)], axis=0)

    def merge(a, b, width):
        return jnp.concatenate([a.reshape(n_p, width), b.reshape(n_s, width)], axis=0)

    for i in range(depth):
        offs = np.cumsum([0, A_QW, A_KW, A_KW, B_QW, B_QW, B_VW, C_KW, C_KW, C_VW, C_GATE_RANK, C_GATE_RANK, C_VW])
        cols = [w_in[i][:, offs[j]:offs[j + 1]] for j in range(12)]
        wa = jnp.concatenate(cols[0:3], axis=1).astype(BF16)
        wb = jnp.concatenate(cols[3:6], axis=1).astype(BF16)
        wc = jnp.concatenate([_pad_cols(cols[6], C_KP), _pad_cols(cols[7], C_KP), cols[8], cols[11],
                              _pad_cols(jnp.concatenate([cols[9], cols[10]], axis=1), 128)], axis=1).astype(BF16)
        w2 = c_gate_w2[i]
        w2f = jnp.zeros((128, C_KP), F32).at[:C_GATE_RANK, :C_KW].set(w2[0]).astype(BF16)
        w2b = jnp.zeros((128, C_KP), F32).at[C_GATE_RANK:2 * C_GATE_RANK, :C_KW].set(w2[1]).astype(BF16)
        gbf = _pad_cols(c_gate_b[i, 0][None, :], C_KP)
        gbb = _pad_cols(c_gate_b[i, 1][None, :], C_KP)
        gq = jnp.tile(a_q_norm[i], A_HEADS)[None, :]
        gk = jnp.tile(a_k_norm[i], A_KV_HEADS)[None, :]

        (qat, ka, vat, qbt, kb, vbt, cq, ck, cv, gf, gb, og, bn) = _inproj(
            x, wa, wb, wc, w2f, w2b, gbf, gbb, gq, gk, e64, e32, cos_t, sin_t, pos_block)

        ka_p, ka_s = group_views(ka, A_KW)
        oa = merge_t(_attn_a(qat, ka_p, vat, bp, first_p, tm), _attn_a(qat, ka_s, vat, bs, first_s, tm), A_QW)

        lam_init = 0.8 - 0.6 * math.exp(-0.3 * i)
        lv = b_lambda[i].astype(F32)
        lam = (jnp.exp(jnp.sum(lv[0] * lv[1])) - jnp.exp(jnp.sum(lv[2] * lv[3])) + lam_init).reshape(1, 1)
        gain_col = b_subln[i].reshape(B_V_DIM, 1)
        kb_p, kb_s = group_views(kb, B_QW)
        bn_p, bn_s = bn[:np_tiles * 8], bn[np_tiles * 8:]
        ob = merge_t(_attn_b(_alibi_window(bn_p, bp, tp, min(ATT_TQ, tp), tm), lam, qbt, kb_p, vbt, tpos(tm), gain_col,
                             1.0 - lam_init, bp, first_p, tm),
                     _attn_b(_alibi_window(bn_s, bs, ts, min(ATT_TQ, ts), tm), lam, qbt, kb_s, vbt, tpos(tm), gain_col,
                             1.0 - lam_init, bs, first_s, tm), B_VW)

        cq_p, cq_s = group_views(cq, C_KP)
        ck_p, ck_s = group_views(ck, C_KP)
        cv_p, cv_s = group_views(cv, C_VW)
        gf_p, gf_s = group_views(gf, C_KP)
        gb_p, gb_s = group_views(gb, C_KP)
        ocf_p, ocb_p = _gla(cq_p, ck_p, cv_p, gf_p, gb_p, bp, first_p)
        ocf_s, ocb_s = _gla(cq_s, ck_s, cv_s, gf_s, gb_s, bs, first_s)
        ocf = merge(ocf_p, ocf_s, C_VW)
        ocb = merge(ocb_p, ocb_s, C_VW)

        wo = w_out[i].astype(BF16)
        wr = jnp.zeros((ROUTE_ROWS, D_MODEL), F32)
        wr = wr.at[0:N_GROUPS].set(w_router_coarse[i].T).at[8:].set(w_router_fine[i].T)
        wrh = wr.astype(BF16)
        wrl = (wr - wrh.astype(F32)).astype(BF16)
        br = jnp.zeros((ROUTE_ROWS, 1), F32)
        br = br.at[0:N_GROUPS, 0].set(b_router_coarse[i].astype(F32)).at[8:, 0].set(b_router_fine[i].astype(F32))
        x1, x1t, eid, gate, hist = _outproj(
            x, oa, ob, ocf, ocb, og, jnp.tile(c_norm[i], C_HEADS)[None, :], e96,
            wo[:A_QW], wo[A_QW:A_QW + B_VW], wo[A_QW + B_VW:], ln1_g[i][None, :], ln1_b[i][None, :],
            wrh, wrl, br, alpha)

        block_expert, slot_tok, slot_dst, slot_gate = _route_slots(eid, gate, hist, n_tok)
        y2 = _moe(block_expert, slot_tok, slot_dst, slot_gate, x1t, w_exp_gate[i].astype(BF16),
                  w_exp_up[i].astype(BF16), w_exp_down[i].astype(BF16), slot_gate.shape[0])

        p_all = jnp.concatenate([p_prompt[i].reshape(n_p, PLE_DIM), p_sample[i].reshape(n_s, PLE_DIM)], axis=0)
        x = _final(x1, y2, p_all, w_ple_gate[i].astype(BF16), w_ple_proj[i].astype(BF16),
                   ln2_g[i][None, :], ln2_b[i][None, :], alpha)

    return x[:n_p].reshape(bp, tp, D_MODEL), x[n_p:].reshape(bs, ts, D_MODEL)
```

```python
import functools
import math

import numpy as np
import jax
import jax.numpy as jnp
from jax import lax
from jax.experimental import pallas as pl
from jax.experimental.pallas import tpu as pltpu

F32 = jnp.float32
BF16 = jnp.bfloat16

D_MODEL = 1024
GRID_W = 64
HEAD_DIM = 64
NORM_EPS = 1e-6
A_HEADS = 6
A_KV_HEADS = 2
A_GROUP = A_HEADS // A_KV_HEADS
ROPE_THETA = 10000.0
B_HEADS = 4
B_QK_DIM = 32
B_V_DIM = 64
B_MAPS = 2 * B_HEADS
C_HEADS = 4
C_K_DIM = 48
C_V_DIM = 96
C_GATE_RANK = 16
C_GATE_TAU = 16.0
C_CHUNK = 64
C_KW = C_HEADS * C_K_DIM
C_KP = 256
C_VW = C_HEADS * C_V_DIM
N_GROUPS = 4
EXPERTS_PER_GROUP = 8
N_EXPERTS = N_GROUPS * EXPERTS_PER_GROUP
D_EXPERT = 512
PLE_DIM = 256
A_QW = A_HEADS * HEAD_DIM
A_KW = A_KV_HEADS * HEAD_DIM
B_QW = B_HEADS * 2 * B_QK_DIM
B_VW = B_HEADS * B_V_DIM
ROUTE_ROWS = 8 + N_EXPERTS

VMEM_LIMIT = 56 * 1024 * 1024

ROW_TILE = 512
ATT_TQ = 1024
ATT_COL = 256
ATT_KS_A = 512
ATT_KS_B = 512
ATT_VROWS = HEAD_DIM + 16
LOG2E = math.log2(math.e)
GLA_BLOCK = 512
GLA_PAR = 2
MOE_BM = 256


def _cparams(sem):
    return pltpu.CompilerParams(dimension_semantics=sem, vmem_limit_bytes=VMEM_LIMIT)


def _full(shape):
    nd = len(shape)
    return pl.BlockSpec(shape, lambda *_: (0,) * nd)


def _dot(a, b):
    return jnp.dot(a, b, preferred_element_type=F32)


def _dot_nt(a, b):
    return lax.dot_general(a, b, (((1,), (1,)), ((), ())), preferred_element_type=F32)


def _dot_tn(a, b):
    return lax.dot_general(a, b, (((0,), (0,)), ((), ())), preferred_element_type=F32)


def _split2(x):
    hi = x.astype(BF16)
    lo = (x - hi.astype(F32)).astype(BF16)
    return hi, lo


def _split3(x):
    hi = x.astype(BF16)
    r = x - hi.astype(F32)
    mid = r.astype(BF16)
    lo = (r - mid.astype(F32)).astype(BF16)
    return hi, mid, lo


def _seg_mean(sq, e):
    hi, lo = _split2(sq)
    return _dot(hi, e) + _dot(lo, e)


def _block_diag_avg(width, seg):
    idx = np.arange(width) // seg
    return jnp.asarray((idx[:, None] == idx[None, :]).astype(np.float32) / seg, dtype=BF16)


def _rope(x, cos, sin_signed):
    pieces = []
    for c in range(x.shape[1] // 128):
        xc = x[:, c * 128:(c + 1) * 128]
        nxt = pltpu.roll(xc, 127, 1)
        prv = pltpu.roll(xc, 1, 1)
        lane = lax.broadcasted_iota(jnp.int32, xc.shape, 1)
        sw = jnp.where(lane % 2 == 0, nxt, prv)
        pieces.append(xc * cos + sw * sin_signed)
    return pieces[0] if len(pieces) == 1 else jnp.concatenate(pieces, axis=1)


def _store_vt(vt_ref, v, n_heads):
    vt = v.T
    for h in range(n_heads):
        vt_ref[0, h * ATT_VROWS:h * ATT_VROWS + HEAD_DIM, :] = vt[h * HEAD_DIM:(h + 1) * HEAD_DIM, :].astype(BF16)
        vt_ref[0, h * ATT_VROWS + HEAD_DIM:(h + 1) * ATT_VROWS, :] = jnp.ones(
            (ATT_VROWS - HEAD_DIM, vt.shape[1]), BF16)


def _inproj_kernel(x_ref, wa_ref, wb_ref, wc_ref, w2f_ref, w2b_ref, gbf_ref, gbb_ref, gq_ref, gk_ref,
                   e64_ref, e32_ref, cos_ref, sin_ref,
                   qat_ref, ka_ref, vat_ref, qbt_ref, kb_ref, vbt_ref, cq_ref, ck_ref, cv_ref, gf_ref, gb_ref,
                   og_ref, bn_ref):
    xb = x_ref[...].astype(BF16)
    cos = cos_ref[...]
    sin = sin_ref[...]

    za = _dot(xb, wa_ref[...])
    q = za[:, :A_QW]
    k = za[:, A_QW:A_QW + A_KW]
    e64 = e64_ref[...]
    qn = q * lax.rsqrt(_seg_mean(q * q, e64) + NORM_EPS) * gq_ref[...]
    kn = k * lax.rsqrt(_seg_mean(k * k, e64[:A_KW, :A_KW]) + NORM_EPS) * gk_ref[...]
    qat_ref[...] = (_rope(qn, cos, sin) * (HEAD_DIM ** -0.5 * LOG2E)).T.astype(BF16)
    ka_ref[...] = _rope(kn, cos, sin).astype(BF16)
    _store_vt(vat_ref, za[:, A_QW + A_KW:], A_KV_HEADS)

    zb = _dot(xb, wb_ref[...])
    qb_f = zb[:, :B_QW] * (B_QK_DIM ** -0.5 * LOG2E)
    qb = qb_f.astype(BF16)
    kb = zb[:, B_QW:2 * B_QW].astype(BF16)
    qbt_ref[...] = qb_f.T.astype(BF16)
    kb_ref[...] = kb
    _store_vt(vbt_ref, zb[:, 2 * B_QW:], B_HEADS)

    def max_sqnorm(t):
        tf = t.astype(F32)
        hi, lo = _split2(tf * tf)
        return jnp.max(_dot(hi, e32_ref[...]) + _dot(lo, e32_ref[...]), axis=0, keepdims=True)

    r8 = lax.broadcasted_iota(jnp.int32, (8, 128), 0)
    bn_ref[...] = jnp.where(r8 == 0, max_sqnorm(qb), jnp.where(r8 == 1, max_sqnorm(kb), 0.0))

    zc = _dot(xb, wc_ref[...])
    cq_ref[...] = zc[:, :C_KP] * (C_K_DIM ** -0.5)
    ck_ref[...] = zc[:, C_KP:2 * C_KP]
    cv_ref[...] = zc[:, 2 * C_KP:2 * C_KP + C_VW]
    og = zc[:, 2 * C_KP + C_VW:2 * C_KP + 2 * C_VW]
    og_ref[...] = og * (1.0 / (1.0 + jnp.exp(-og)))
    lr = zc[:, 2 * C_KP + 2 * C_VW:].astype(BF16)

    def log_decay(w2_ref, bias_ref):
        g = _dot(lr, w2_ref[...]) + bias_ref[...]
        return (jnp.minimum(g, 0.0) - jnp.log(1.0 + jnp.exp(-jnp.abs(g)))) * (1.0 / C_GATE_TAU)

    gf_ref[...] = log_decay(w2f_ref, gbf_ref)
    gb_ref[...] = log_decay(w2b_ref, gbb_ref)


def _inproj(x, wa, wb, wc, w2f, w2b, gbf, gbb, gq, gk, e64, e32, cos_t, sin_t, pos_block):
    n = x.shape[0]
    tm = ROW_TILE
    row = lambda w: pl.BlockSpec((tm, w), lambda i: (i, 0))
    tab = pl.BlockSpec((tm, 128), lambda i: (pos_block(i), 0))
    colt = lambda w: (pl.BlockSpec((w, tm), lambda i: (0, i)), jax.ShapeDtypeStruct((w, n), BF16))
    vtile = lambda h: (pl.BlockSpec((1, h * ATT_VROWS, tm), lambda i: (i, 0, 0)),
                       jax.ShapeDtypeStruct((n // tm, h * ATT_VROWS, tm), BF16))
    rowo = lambda w, dt: (row(w), jax.ShapeDtypeStruct((n, w), dt))
    outs = [colt(A_QW), rowo(A_KW, BF16), vtile(A_KV_HEADS), colt(B_QW), rowo(B_QW, BF16), vtile(B_HEADS),
            rowo(C_KP, F32), rowo(C_KP, F32), rowo(C_VW, F32), rowo(C_KP, F32), rowo(C_KP, F32), rowo(C_VW, F32),
            (pl.BlockSpec((8, 128), lambda i: (i, 0)), jax.ShapeDtypeStruct((n // tm * 8, 128), F32))]
    return pl.pallas_call(
        _inproj_kernel,
        grid=(n // tm,),
        in_specs=[row(D_MODEL), _full(wa.shape), _full(wb.shape), _full(wc.shape), _full(w2f.shape),
                  _full(w2b.shape), _full(gbf.shape), _full(gbb.shape), _full(gq.shape), _full(gk.shape),
                  _full(e64.shape), _full(e32.shape), tab, tab],
        out_specs=[s for s, _ in outs],
        out_shape=[t for _, t in outs],
        compiler_params=_cparams(("parallel",)),
        name="inproj",
    )(x, wa, wb, wc, w2f, w2b, gbf, gbb, gq, gk, e64, e32, cos_t, sin_t)


def _attn_loop(k_ref, vt_ref, qpad, s_scr, m_ref, acc_ref, ks, vhead_of_map):
    n_maps, _, tq = qpad.shape
    ncol = tq // ATT_COL
    n_sub = k_ref.shape[1] // ks
    m_ref[...] = jnp.full(m_ref.shape, -jnp.inf, F32)
    acc_ref[...] = jnp.zeros(acc_ref.shape, F32)

    k0 = k_ref[0, 0:ks, :]
    for mp in range(n_maps):
        for c in range(ncol):
            s_scr[mp, c] = _dot(k0, qpad[mp, :, c * ATT_COL:(c + 1) * ATT_COL])

    def key_tile(j, carry):
        jn = jnp.minimum(j + 1, n_sub - 1)
        kn = k_ref[0, pl.ds(pl.multiple_of(jn * ks, ks), ks), :]
        for c in range(ncol):
            cs = slice(c * ATT_COL, (c + 1) * ATT_COL)
            for mp in range(n_maps):
                vh = vhead_of_map[mp]
                s = s_scr[mp, c]
                s_scr[mp, c] = _dot(kn, qpad[mp, :, cs])
                m_prev = m_ref[mp, :, cs]
                m_new = jnp.maximum(m_prev, jnp.max(s, axis=0, keepdims=True))
                p = jnp.exp2(s - m_new).astype(BF16)
                vt = vt_ref[j, vh * ATT_VROWS:(vh + 1) * ATT_VROWS, :]
                acc_ref[mp, :, cs] = jnp.exp2(m_prev - m_new) * acc_ref[mp, :, cs] + _dot(vt, p)
                m_ref[mp, :, cs] = m_new
        return carry

    lax.fori_loop(0, n_sub, key_tile, 0)


def _attn_out(acc_ref, mp):
    return acc_ref[mp, 0:HEAD_DIM, :] / acc_ref[mp, HEAD_DIM:HEAD_DIM + 1, :]


def _attn_a_kernel(qt_ref, k_ref, vt_ref, ot_ref, qpad, s_scr, m_ref, acc_ref, *, ks):
    qpad[...] = jnp.zeros(qpad.shape, BF16)
    for h in range(A_HEADS):
        g = h // A_GROUP
        qpad[h, g * HEAD_DIM:(g + 1) * HEAD_DIM, :] = qt_ref[h * HEAD_DIM:(h + 1) * HEAD_DIM, :]
    _attn_loop(k_ref, vt_ref, qpad, s_scr, m_ref, acc_ref, ks, tuple(h // A_GROUP for h in range(A_HEADS)))
    for h in range(A_HEADS):
        ot_ref[0, h * HEAD_DIM:(h + 1) * HEAD_DIM, :] = _attn_out(acc_ref, h).astype(BF16)


def _attn_scratch(n_maps, qk_width, tq, ks):
    return [pltpu.VMEM((n_maps, qk_width, tq), BF16),
            pltpu.VMEM((n_maps, tq // ATT_COL, ks, ATT_COL), F32),
            pltpu.VMEM((n_maps, 1, tq), F32),
            pltpu.VMEM((n_maps, ATT_VROWS, tq), F32)]


def _attn_a(qt, k, vt, bsz, b0, ks):
    seq = k.shape[1]
    tq = min(ATT_TQ, seq)
    nq = seq // tq
    return pl.pallas_call(
        functools.partial(_attn_a_kernel, ks=ks),
        grid=(bsz, nq),
        in_specs=[pl.BlockSpec((A_QW, tq), lambda b, qi: (0, (b + b0) * nq + qi)),
                  pl.BlockSpec((1, seq, A_KW), lambda b, qi: (b + b0, 0, 0), pipeline_mode=pl.Buffered(1)),
                  pl.BlockSpec((seq // ks, A_KV_HEADS * ATT_VROWS, ks), lambda b, qi: (b + b0, 0, 0),
                               pipeline_mode=pl.Buffered(1))],
        out_specs=pl.BlockSpec((1, A_QW, tq), lambda b, qi: (b, 0, qi)),
        out_shape=jax.ShapeDtypeStruct((bsz, A_QW, seq), BF16),
        scratch_shapes=_attn_scratch(A_HEADS, A_KW, tq, ks),
        compiler_params=_cparams(("parallel", "arbitrary")),
        name="attn_a",
    )(qt, k, vt)


def _alibi_slope(h):
    return 2.0 ** (-8.0 * (h + 1) / B_HEADS)


def _attn_b_kernel(bnd_ref, lam_ref, qt_ref, k_ref, vt_ref, tpos_ref, gain_ref, ot_ref, qpad, s_scr, m_ref,
                   acc_ref, *, ks, out_scale):
    b = pl.program_id(0)
    qi = pl.program_id(1)
    tq = qt_ref.shape[1]
    ncol = tq // ATT_COL
    n_sub = k_ref.shape[1] // ks
    n_diag = tq // ks
    d_lo = qi * n_diag

    qpad[...] = jnp.zeros(qpad.shape, BF16)
    for mp in range(B_MAPS):
        qpad[mp, mp * B_QK_DIM:(mp + 1) * B_QK_DIM, :] = qt_ref[mp * B_QK_DIM:(mp + 1) * B_QK_DIM, :]
    m_ref[...] = jnp.full(m_ref.shape, -jnp.inf, F32)
    acc_ref[...] = jnp.zeros(acc_ref.shape, F32)

    def key_rows(j):
        return k_ref[0, pl.ds(pl.multiple_of(j * ks, ks), ks), :]

    for h in range(B_HEADS):
        slope = _alibi_slope(h) * LOG2E
        base = ((b * pl.num_programs(1) + qi) * B_HEADS + h) * 2
        j_lo = bnd_ref[base]
        j_hi = bnd_ref[base + 1]

        k0 = key_rows(j_lo)
        for mi in range(2):
            for c in range(ncol):
                s_scr[mi, c] = _dot(k0, qpad[2 * h + mi, :, c * ATT_COL:(c + 1) * ATT_COL])

        def key_tile(j, carry, side, h=h, slope=slope):
            kn = key_rows(jnp.minimum(j + 1, n_sub - 1))
            vt = vt_ref[j, h * ATT_VROWS:(h + 1) * ATT_VROWS, :]
            for c in range(ncol):
                cs = slice(c * ATT_COL, (c + 1) * ATT_COL)
                delta = qi * tq + c * ATT_COL - j * ks
                if side == 0:
                    rel = (lax.broadcasted_iota(jnp.int32, (ks, ATT_COL), 1)
                           - lax.broadcasted_iota(jnp.int32, (ks, ATT_COL), 0) + delta)
                    bias = jnp.abs(rel).astype(F32) * (-slope)
                    kappa = 0.0
                else:
                    kappa = delta.astype(F32) * (-side * slope)
                for mi in range(2):
                    mp = 2 * h + mi
                    if side == 0:
                        s = s_scr[mi, c] + bias
                    elif side == 1:
                        s = s_scr[mi, c] - tpos_ref[h]
                    else:
                        s = s_scr[mi, c] + tpos_ref[h]
                    s_scr[mi, c] = _dot(kn, qpad[mp, :, cs])
                    m_prev = m_ref[mp, :, cs]
                    m_new = jnp.maximum(m_prev, jnp.max(s, axis=0, keepdims=True) + kappa)
                    p = jnp.exp2(s - (m_new - kappa)).astype(BF16)
                    acc_ref[mp, :, cs] = jnp.exp2(m_prev - m_new) * acc_ref[mp, :, cs] + _dot(vt, p)
                    m_ref[mp, :, cs] = m_new
            return carry

        lax.fori_loop(j_lo, d_lo, functools.partial(key_tile, side=1), 0)
        for dj in range(n_diag):
            key_tile(d_lo + dj, 0, 0)
        lax.fori_loop(d_lo + n_diag, j_hi, functools.partial(key_tile, side=-1), 0)

    lam = lam_ref[0, 0]
    for h in range(B_HEADS):
        o = _attn_out(acc_ref, 2 * h) - lam * _attn_out(acc_ref, 2 * h + 1)
        ms = jnp.mean(o * o, axis=0, keepdims=True)
        o = o * lax.rsqrt(ms + NORM_EPS) * gain_ref[...] * out_scale
        ot_ref[0, h * B_V_DIM:(h + 1) * B_V_DIM, :] = o.astype(BF16)


def _alibi_window(bn_tiles, bsz, seq, tq, ks):
    nq = seq // tq
    n_sub = seq // ks
    t = bn_tiles.reshape(bsz, nq, -1, 8, 128)
    q2 = t[:, :, :, 0, :B_MAPS].max(axis=2).reshape(bsz, nq, B_HEADS, 2).max(axis=-1)
    k2 = t[:, :, :, 1, :B_MAPS].max(axis=(1, 2)).reshape(bsz, B_HEADS, 2).max(axis=-1)
    u = jnp.sqrt(q2 * k2[:, None, :]) * 1.01
    slope = jnp.asarray([_alibi_slope(h) * LOG2E for h in range(B_HEADS)], F32)
    reach = (2.0 * u + 152.0) / slope
    w = jnp.minimum(jnp.floor((reach + (ks - 1)) / ks), n_sub).astype(jnp.int32)
    d_lo = jnp.arange(nq, dtype=jnp.int32)[None, :, None] * (tq // ks)
    return jnp.stack([jnp.maximum(d_lo - w, 0), jnp.minimum(d_lo + tq // ks + w, n_sub)], axis=-1).reshape(-1)


def _attn_b(bounds, lam, qt, k, vt, tpos, gain_col, out_scale, bsz, b0, ks):
    seq = k.shape[1]
    tq = min(ATT_TQ, seq)
    nq = seq // tq
    assert tq % ks == 0
    grid_spec = pltpu.PrefetchScalarGridSpec(
        num_scalar_prefetch=1,
        grid=(bsz, nq),
        in_specs=[pl.BlockSpec((1, 1), lambda b, qi, bnd: (0, 0), memory_space=pltpu.SMEM),
                  pl.BlockSpec((B_QW, tq), lambda b, qi, bnd: (0, (b + b0) * nq + qi)),
                  pl.BlockSpec((1, seq, B_QW), lambda b, qi, bnd: (b + b0, 0, 0), pipeline_mode=pl.Buffered(1)),
                  pl.BlockSpec((seq // ks, B_HEADS * ATT_VROWS, ks), lambda b, qi, bnd: (b + b0, 0, 0),
                               pipeline_mode=pl.Buffered(1)),
                  pl.BlockSpec((B_HEADS, ks, ATT_COL), lambda b, qi, bnd: (0, 0, 0), pipeline_mode=pl.Buffered(1)),
                  pl.BlockSpec((B_V_DIM, 1), lambda b, qi, bnd: (0, 0))],
        out_specs=pl.BlockSpec((1, B_VW, tq), lambda b, qi, bnd: (b, 0, qi)),
        scratch_shapes=[pltpu.VMEM((B_MAPS, B_QW, tq), BF16),
                        pltpu.VMEM((2, tq // ATT_COL, ks, ATT_COL), F32),
                        pltpu.VMEM((B_MAPS, 1, tq), F32),
                        pltpu.VMEM((B_MAPS, ATT_VROWS, tq), F32)],
    )
    return pl.pallas_call(
        functools.partial(_attn_b_kernel, ks=ks, out_scale=out_scale),
        grid_spec=grid_spec,
        out_shape=jax.ShapeDtypeStruct((bsz, B_VW, seq), BF16),
        compiler_params=_cparams(("parallel", "arbitrary")),
        name="attn_b",
    )(bounds, lam, qt, k, vt, tpos, gain_col)


def _gla_masks():
    j = np.arange(C_CHUNK)
    tri_f = (j[None, :] <= j[:, None]).astype(np.float32)
    tri_b = (j[None, :] >= j[:, None]).astype(np.float32)
    d = np.arange(C_KP)
    dhead = np.where(d < C_KW, d // C_K_DIM, -1)
    hj = np.arange(C_HEADS * C_CHUNK) // C_CHUNK
    vhead = np.arange(C_VW) // C_V_DIM
    m_stack = (hj[:, None] == dhead[None, :]).astype(np.float32)
    m_v = (hj[:, None] == vhead[None, :]).astype(np.float32)
    m_s = (vhead[:, None] == dhead[None, :]).astype(np.float32)
    return (jnp.asarray(np.stack([tri_f, tri_b]), BF16),
            jnp.asarray(np.stack([np.tile(tri_f, (1, C_HEADS)), np.tile(tri_b, (1, C_HEADS))]), F32),
            jnp.asarray(m_stack, F32), jnp.asarray(m_v, F32), jnp.asarray(m_s, F32))


def _gla_kernel(qf_ref, kf_ref, vf_ref, gf_ref, qb_ref, kb_ref, vb_ref, gb_ref, tri_ref, mp_ref, mstack_ref,
                mv_ref, ms_ref, of_ref, ob_ref, st_ref):
    @pl.when(pl.program_id(1) == 0)
    def _():
        st_ref[...] = jnp.zeros(st_ref.shape, F32)

    nb = qf_ref.shape[0]
    n_chunks = qf_ref.shape[1] // C_CHUNK
    chains = [(d, e) for e in range(nb) for d in range(2)]
    srcs = ((qf_ref, kf_ref, vf_ref, gf_ref, of_ref), (qb_ref, kb_ref, vb_ref, gb_ref, ob_ref))

    def chunk(ci, carry):
        rows = [pl.ds(pl.multiple_of((ci if d == 0 else n_chunks - 1 - ci) * C_CHUNK, C_CHUNK), C_CHUNK)
                for d, _ in chains]
        bs = []
        for (d, e), r in zip(chains, rows):
            g_hi, g_mid, g_lo = _split3(srcs[d][3][e, r, :])
            tri = tri_ref[d]
            bs.append(_dot(tri, g_hi) + _dot(tri, g_mid) + _dot(tri, g_lo))
        aa, khats, ps, edges = [], [], [], []
        for (d, e), r, b in zip(chains, rows, bs):
            edge = C_CHUNK - 1 if d == 0 else 0
            b_edge = b[edge:edge + 1, :]
            q = srcs[d][0][e, r, :]
            k = srcs[d][1][e, r, :]
            a = (q * jnp.exp(b)).astype(BF16)
            bm = k * jnp.exp(-b)
            khats.append((k * jnp.exp(b_edge - b)).astype(BF16))
            bm_stack = (jnp.concatenate([bm] * C_HEADS, axis=0) * mstack_ref[...]).astype(BF16)
            ps.append((_dot_nt(a, bm_stack) * mp_ref[d]).astype(BF16))
            aa.append(a)
            edges.append(jnp.exp(b_edge))
        for ix, ((d, e), r) in enumerate(zip(chains, rows)):
            v = srcs[d][2][e, r, :]
            v_bd = (jnp.concatenate([v] * C_HEADS, axis=0) * mv_ref[...]).astype(BF16)
            st = st_ref[ix]
            srcs[d][4][e, r, :] = _dot(ps[ix], v_bd) + _dot_nt(aa[ix], st.astype(BF16))
            st_ref[ix] = st * edges[ix] + _dot_tn(v.astype(BF16), khats[ix]) * ms_ref[...]
        return carry

    lax.fori_loop(0, n_chunks, chunk, 0)


def _gla(q, k, v, gf, gb, bsz, b0):
    seq = q.shape[1]
    blk = min(GLA_BLOCK, seq)
    nblk = seq // blk
    nb = GLA_PAR if bsz % GLA_PAR == 0 and b0 % GLA_PAR == 0 else 1
    masks = _gla_masks()
    fmap = lambda b, t: (b + b0 // nb, t, 0)
    bmap = lambda b, t: (b + b0 // nb, nblk - 1 - t, 0)
    omap = lambda b, t: (b, t, 0)
    ormap = lambda b, t: (b, nblk - 1 - t, 0)
    kw = lambda m: pl.BlockSpec((nb, blk, C_KP), m)
    vw = lambda m: pl.BlockSpec((nb, blk, C_VW), m)
    return pl.pallas_call(
        _gla_kernel,
        grid=(bsz // nb, nblk),
        in_specs=[kw(fmap), kw(fmap), vw(fmap), kw(fmap), kw(bmap), kw(bmap), vw(bmap), kw(bmap)]
                 + [_full(m.shape) for m in masks],
        out_specs=[vw(omap), vw(ormap)],
        out_shape=[jax.ShapeDtypeStruct((bsz, seq, C_VW), F32)] * 2,
        scratch_shapes=[pltpu.VMEM((2 * nb, C_VW, C_KP), F32)],
        compiler_params=_cparams(("parallel", "arbitrary")),
        name="gla",
    )(q, k, v, gf, q, k, v, gb, *masks)


def _layer_norm(x, g, b):
    xc = x - jnp.mean(x, axis=-1, keepdims=True)
    var = jnp.mean(xc * xc, axis=-1, keepdims=True)
    return xc * lax.rsqrt(var + NORM_EPS) * g + b


def _outproj_kernel(x_ref, oap_ref, oas_ref, obp_ref, obs_ref, ocfp_ref, ocfs_ref, ocbp_ref, ocbs_ref, og_ref,
                    cg_ref, e96_ref, woa_ref, wob_ref, woc_ref, g1_ref, b1_ref, wrh_ref, wrl_ref, br_ref,
                    x1_ref, x1t_ref, eid_ref, gate_ref, hist_ref, *, alpha, np_tiles):
    in_prompt = pl.program_id(0) < np_tiles
    oa_t = jnp.where(in_prompt, oap_ref[0], oas_ref[0])
    ob_t = jnp.where(in_prompt, obp_ref[0], obs_ref[0])
    oc = jnp.where(in_prompt, ocfp_ref[0] + ocbp_ref[0], ocfs_ref[0] + ocbs_ref[0])
    oc = oc * lax.rsqrt(_seg_mean(oc * oc, e96_ref[...]) + NORM_EPS) * cg_ref[...] * og_ref[...]
    mixed = (_dot_tn(oa_t, woa_ref[...]) + _dot_tn(ob_t, wob_ref[...])
             + _dot(oc.astype(BF16), woc_ref[...]))
    x1 = _layer_norm(alpha * x_ref[...] + mixed, g1_ref[...], b1_ref[...])
    x1_ref[...] = x1
    for c in range(D_MODEL // 128):
        x1t_ref[pl.ds(c, x1.shape[0], stride=8), :] = x1[:, c * 128:(c + 1) * 128]

    x_hi, x_lo = _split2(x1)
    wrh = wrh_ref[...]
    lt = _dot_nt(wrh, x_hi) + _dot_nt(wrh, x_lo) + _dot_nt(wrl_ref[...], x_hi) + br_ref[...]
    tm = lt.shape[1]
    coarse = lt[0:N_GROUPS]
    r4 = lax.broadcasted_iota(jnp.int32, (N_GROUPS, tm), 0)
    cmax = jnp.max(coarse, axis=0, keepdims=True)
    group = jnp.min(jnp.where(coarse == cmax, r4, N_GROUPS), axis=0, keepdims=True)
    group_w = 1.0 / jnp.sum(jnp.exp(coarse - cmax), axis=0, keepdims=True)
    fine = jnp.zeros((EXPERTS_PER_GROUP, tm), F32)
    for g in range(N_GROUPS):
        fine = jnp.where(group == g, lt[8 + 8 * g:16 + 8 * g], fine)
    r8 = lax.broadcasted_iota(jnp.int32, (EXPERTS_PER_GROUP, tm), 0)
    v1 = jnp.max(fine, axis=0, keepdims=True)
    i1 = jnp.min(jnp.where(fine == v1, r8, EXPERTS_PER_GROUP), axis=0, keepdims=True)
    rest = jnp.where(r8 == i1, -jnp.inf, fine)
    v2 = jnp.max(rest, axis=0, keepdims=True)
    i2 = jnp.min(jnp.where(rest == v2, r8, EXPERTS_PER_GROUP), axis=0, keepdims=True)
    e2 = jnp.exp(v2 - v1)
    w1 = group_w / (1.0 + e2)
    w2 = group_w * e2 / (1.0 + e2)
    e1 = group * EXPERTS_PER_GROUP + i1
    e2nd = group * EXPERTS_PER_GROUP + i2
    eid_ref[...] = jnp.where(r8 == 0, e1, jnp.where(r8 == 1, e2nd, 0))
    gate_ref[...] = jnp.where(r8 == 0, w1, jnp.where(r8 == 1, w2, 0.0))
    r32 = lax.broadcasted_iota(jnp.int32, (N_EXPERTS, tm), 0)
    hits = jnp.where(r32 == e1, 1.0, 0.0) + jnp.where(r32 == e2nd, 1.0, 0.0)
    hist_ref[...] = jnp.broadcast_to(jnp.sum(hits, axis=1, keepdims=True), (N_EXPERTS, 128)).astype(jnp.int32)


def _outproj(x, oa, ob, ocf, ocb, og, cg, e96, woa, wob, woc, g1, b1, wrh, wrl, br, alpha):
    n = x.shape[0]
    tm = ROW_TILE
    np_tiles = oa[0].shape[0] * oa[0].shape[2] // tm
    row = lambda w: pl.BlockSpec((tm, w), lambda i: (i, 0))
    col = pl.BlockSpec((8, tm), lambda i: (0, i))

    def group_specs(arrs, transposed):
        specs = []
        for g, a in enumerate(arrs):
            per_seq = a.shape[2 if transposed else 1] // tm
            last = a.shape[0] * per_seq - 1

            def imap(i, g=g, per_seq=per_seq, last=last):
                t = jnp.clip(i - g * np_tiles, 0, last)
                return (t // per_seq, 0, t % per_seq) if transposed else (t // per_seq, t % per_seq, 0)
            shape = (1, a.shape[1], tm) if transposed else (1, tm, a.shape[2])
            specs.append(pl.BlockSpec(shape, imap))
        return specs

    return pl.pallas_call(
        functools.partial(_outproj_kernel, alpha=alpha, np_tiles=np_tiles),
        grid=(n // tm,),
        in_specs=[row(D_MODEL)] + group_specs(oa, True) + group_specs(ob, True) + group_specs(ocf, False)
                 + group_specs(ocb, False)
                 + [row(C_VW), _full(cg.shape),
                  _full(e96.shape), _full(woa.shape), _full(wob.shape), _full(woc.shape), _full(g1.shape),
                  _full(b1.shape), _full(wrh.shape), _full(wrl.shape), _full(br.shape)],
        out_specs=[row(D_MODEL), pl.BlockSpec((tm * 8, 128), lambda i: (i, 0)), col, col,
                   pl.BlockSpec((N_EXPERTS, 128), lambda i: (i, 0))],
        out_shape=[jax.ShapeDtypeStruct((n, D_MODEL), F32), jax.ShapeDtypeStruct((n * 8, 128), F32),
                   jax.ShapeDtypeStruct((8, n), jnp.int32),
                   jax.ShapeDtypeStruct((8, n), F32),
                   jax.ShapeDtypeStruct((n // tm * N_EXPERTS, 128), jnp.int32)],
        compiler_params=_cparams(("parallel",)),
        name="outproj",
    )(x, *oa, *ob, *ocf, *ocb, og, cg, e96, woa, wob, woc, g1, b1, wrh, wrl, br)


def _moe_kernel(be_ref, tok_ref, tokn_ref, dst_ref, gate_ref, x_hbm, wg_ref, wu_ref, wd_ref, y_hbm,
                xbuf, ybuf, gsem, ssem):
    i = pl.program_id(0)
    nb = pl.num_programs(0)
    bm = xbuf.shape[1] // 8
    slot = i % 2

    def token_rows(t):
        return pl.ds(pl.multiple_of(t * 8, 8), 8)

    def gather(idx_ref, sl, start):
        for r in range(bm):
            cp = pltpu.make_async_copy(x_hbm.at[token_rows(idx_ref[0, 0, r]), :], xbuf.at[sl, pl.ds(r * 8, 8), :],
                                       gsem.at[sl])
            if start:
                cp.start()
            else:
                cp.wait()

    def scatter(sl, start):
        for r in range(bm):
            cp = pltpu.make_async_copy(ybuf.at[sl, pl.ds(r * 8, 8), :], y_hbm.at[token_rows(dst_ref[0, 0, r]), :],
                                       ssem.at[sl])
            if start:
                cp.start()
            else:
                cp.wait()

    @pl.when(i == 0)
    def _():
        gather(tok_ref, 0, True)

    @pl.when(i >= 2)
    def _():
        scatter(slot, False)

    gather(tok_ref, slot, False)
    gather(tokn_ref, 1 - slot, True)

    xb = jnp.concatenate([xbuf[slot, pl.ds(c, bm, stride=8), :] for c in range(D_MODEL // 128)], axis=1).astype(BF16)
    hg = _dot(xb, wg_ref[0])
    hu = _dot(xb, wu_ref[0])
    h = (hg * (1.0 / (1.0 + jnp.exp(-hg))) * hu).astype(BF16)
    y = _dot(h, wd_ref[0]) * gate_ref[...]
    for c in range(D_MODEL // 128):
        ybuf[slot, pl.ds(c, bm, stride=8), :] = y[:, c * 128:(c + 1) * 128]
    scatter(slot, True)

    @pl.when(i == nb - 1)
    def _():
        scatter(slot, False)
        gather(tok_ref, 1 - slot, False)

        @pl.when(nb >= 2)
        def _():
            scatter(1 - slot, False)


def _moe(block_expert, slot_tok, slot_dst, slot_gate, x1, wg, wu, wd, n_rows_out):
    nb = slot_tok.shape[0]
    bm = MOE_BM
    grid_spec = pltpu.PrefetchScalarGridSpec(
        num_scalar_prefetch=1,
        grid=(nb,),
        in_specs=[pl.BlockSpec((1, 1, bm), lambda i, be: (i, 0, 0), memory_space=pltpu.SMEM),
                  pl.BlockSpec((1, 1, bm), lambda i, be: (jnp.minimum(i + 1, nb - 1), 0, 0),
                               memory_space=pltpu.SMEM),
                  pl.BlockSpec((1, 1, bm), lambda i, be: (i, 0, 0), memory_space=pltpu.SMEM),
                  pl.BlockSpec((bm, 1), lambda i, be: (i, 0)),
                  pl.BlockSpec(memory_space=pl.ANY),
                  pl.BlockSpec((1, D_MODEL, D_EXPERT), lambda i, be: (be[i], 0, 0)),
                  pl.BlockSpec((1, D_MODEL, D_EXPERT), lambda i, be: (be[i], 0, 0)),
                  pl.BlockSpec((1, D_EXPERT, D_MODEL), lambda i, be: (be[i], 0, 0))],
        out_specs=pl.BlockSpec(memory_space=pl.ANY),
        scratch_shapes=[pltpu.VMEM((2, bm * 8, 128), F32), pltpu.VMEM((2, bm * 8, 128), F32),
                        pltpu.SemaphoreType.DMA((2,)), pltpu.SemaphoreType.DMA((2,))],
    )
    return pl.pallas_call(
        _moe_kernel,
        grid_spec=grid_spec,
        out_shape=jax.ShapeDtypeStruct((n_rows_out * 8, 128), F32),
        compiler_params=_cparams(("arbitrary",)),
        name="moe_ffn",
    )(block_expert, slot_tok, slot_tok, slot_dst, slot_gate, x1, wg, wu, wd)


def _route_slots(eid, gate, hist, n_tok):
    bm = MOE_BM
    n_assign = 2 * n_tok
    flat_e = eid[0:2].reshape(-1)
    flat_g = gate[0:2].reshape(-1)
    order = jnp.argsort(flat_e).astype(jnp.int32)
    counts = jnp.sum(hist.reshape(-1, N_EXPERTS, 128)[:, :, 0], axis=0)
    seg_end = jnp.cumsum(counts)
    seg_start = seg_end - counts
    padded = (counts + bm - 1) // bm * bm
    pad_end = jnp.cumsum(padded)
    pad_start = pad_end - padded
    nb = n_assign // bm + N_EXPERTS
    n_slots = nb * bm
    block_start = jnp.arange(nb, dtype=jnp.int32) * bm
    block_expert = jnp.minimum(jnp.sum(block_start[:, None] >= pad_end[None, :], axis=1), N_EXPERTS - 1
                               ).astype(jnp.int32)
    slot = block_start[:, None] + jnp.arange(bm, dtype=jnp.int32)[None, :]
    rank = slot - pad_start[block_expert][:, None]
    valid = rank < counts[block_expert][:, None]
    src = order[jnp.where(valid, seg_start[block_expert][:, None] + rank, 0)]
    slot_tok = jnp.where(valid, src % n_tok, 0)
    slot_dst = jnp.where(valid, src, n_assign + slot - seg_end[block_expert][:, None])
    slot_gate = jnp.where(valid, flat_g[src], 0.0)
    return (block_expert, slot_tok.reshape(nb, 1, bm), slot_dst.reshape(nb, 1, bm), slot_gate.reshape(n_slots, 1))


def _final_kernel(x1_ref, y0_ref, y1_ref, p_ref, wpg_ref, wpp_ref, g2_ref, b2_ref, o_ref, *, alpha):
    x1 = x1_ref[...]
    gate = 1.0 / (1.0 + jnp.exp(-_dot(x1.astype(BF16), wpg_ref[...])))
    ple = gate * _dot(p_ref[...].astype(BF16), wpp_ref[...])
    tm = x1.shape[0]
    ffn = jnp.concatenate([y0_ref[pl.ds(c, tm, stride=8), :] + y1_ref[pl.ds(c, tm, stride=8), :]
                           for c in range(D_MODEL // 128)], axis=1)
    o_ref[...] = _layer_norm(alpha * x1 + ffn + ple, g2_ref[...], b2_ref[...])


def _final(x1, y2, p, wpg, wpp, g2, b2, alpha):
    n = x1.shape[0]
    tm = ROW_TILE
    nt = n // tm
    row = lambda w: pl.BlockSpec((tm, w), lambda i: (i, 0))
    return pl.pallas_call(
        functools.partial(_final_kernel, alpha=alpha),
        grid=(nt,),
        in_specs=[row(D_MODEL), pl.BlockSpec((tm * 8, 128), lambda i: (i, 0)),
                  pl.BlockSpec((tm * 8, 128), lambda i: (i + nt, 0)), row(PLE_DIM),
                  _full(wpg.shape), _full(wpp.shape), _full(g2.shape), _full(b2.shape)],
        out_specs=row(D_MODEL),
        out_shape=jax.ShapeDtypeStruct((n, D_MODEL), F32),
        compiler_params=_cparams(("parallel",)),
        name="final",
    )(x1, y2, y2, p, wpg, wpp, g2, b2)


def _rope_tables(seq_len):
    rows = seq_len // GRID_W
    row = jnp.repeat(jnp.arange(rows, dtype=F32), GRID_W)
    col = jnp.tile(jnp.arange(GRID_W, dtype=F32), rows)
    n_pairs = HEAD_DIM // 4
    inv_freq = ROPE_THETA ** (-jnp.arange(n_pairs, dtype=F32) / n_pairs)
    ang = jnp.concatenate([row[:, None] * inv_freq, col[:, None] * inv_freq], axis=-1)
    cos = jnp.repeat(jnp.cos(ang), 2, axis=-1)
    sin = jnp.repeat(jnp.sin(ang), 2, axis=-1) * jnp.tile(jnp.asarray([-1.0, 1.0], F32), HEAD_DIM // 2)
    return jnp.tile(cos, (1, 2)), jnp.tile(sin, (1, 2))


def _pad_cols(w, width):
    return jnp.pad(w, ((0, 0), (0, width - w.shape[1])))


def kernel(x_prompt, x_sample, p_prompt, p_sample, w_in, a_q_norm, a_k_norm, b_lambda, b_subln, c_gate_w2,
           c_gate_b, c_norm, w_out, ln1_g, ln1_b, w_router_coarse, b_router_coarse, w_router_fine, b_router_fine,
           w_exp_gate, w_exp_up, w_exp_down, w_ple_gate, w_ple_proj, ln2_g, ln2_b):
    depth = w_in.shape[0]
    alpha = (2.0 * depth) ** 0.25
    bp, tp, _ = x_prompt.shape
    bs, ts, _ = x_sample.shape
    n_p, n_s = bp * tp, bs * ts
    n_tok = n_p + n_s
    tm = ROW_TILE
    assert tp % tm == 0 and ts % tm == 0 and tp % GRID_W == 0 and ts % GRID_W == 0
    assert (2 * n_tok) % MOE_BM == 0
    assert n_tok % tp == 0 and n_tok % ts == 0 and n_p % ts == 0
    assert ATT_KS_A == tm and ATT_KS_B == tm

    cos_t, sin_t = _rope_tables(max(tp, ts))
    np_tiles, tp_tiles, ts_tiles = n_p // tm, tp // tm, ts // tm

    def pos_block(i):
        return jnp.where(i < np_tiles, i % tp_tiles, (i - np_tiles) % ts_tiles)

    e64 = _block_diag_avg(A_QW, HEAD_DIM)
    e96 = _block_diag_avg(C_VW, C_V_DIM)
    e32 = jnp.asarray(np.arange(B_QW)[:, None] // B_QK_DIM == np.arange(128)[None, :], BF16)

    def tpos(ks):
        li = (np.arange(ATT_COL)[None, :] - np.arange(ks)[:, None]).astype(np.float32)
        return jnp.stack([jnp.asarray(li) * jnp.float32(_alibi_slope(h) * LOG2E) for h in range(B_HEADS)])
    x = jnp.concatenate([x_prompt.reshape(n_p, D_MODEL), x_sample.reshape(n_s, D_MODEL)], axis=0)

    def group_views(t, width):
        return t.reshape(n_tok // tp, tp, width), t.reshape(n_tok // ts, ts, width)

    first_p, first_s = 0, n_p // ts

    for i in range(depth):
        offs = np.cumsum([0, A_QW, A_KW, A_KW, B_QW, B_QW, B_VW, C_KW, C_KW, C_VW, C_GATE_RANK, C_GATE_RANK, C_VW])
        cols = [w_in[i][:, offs[j]:offs[j + 1]] for j in range(12)]
        wa = jnp.concatenate(cols[0:3], axis=1).astype(BF16)
        wb = jnp.concatenate(cols[3:6], axis=1).astype(BF16)
        wc = jnp.concatenate([_pad_cols(cols[6], C_KP), _pad_cols(cols[7], C_KP), cols[8], cols[11],
                              _pad_cols(jnp.concatenate([cols[9], cols[10]], axis=1), 128)], axis=1).astype(BF16)
        w2 = c_gate_w2[i]
        w2f = jnp.zeros((128, C_KP), F32).at[:C_GATE_RANK, :C_KW].set(w2[0]).astype(BF16)
        w2b = jnp.zeros((128, C_KP), F32).at[C_GATE_RANK:2 * C_GATE_RANK, :C_KW].set(w2[1]).astype(BF16)
        gbf = _pad_cols(c_gate_b[i, 0][None, :], C_KP)
        gbb = _pad_cols(c_gate_b[i, 1][None, :], C_KP)
        gq = jnp.tile(a_q_norm[i], A_HEADS)[None, :]
        gk = jnp.tile(a_k_norm[i], A_KV_HEADS)[None, :]

        (qat, ka, vat, qbt, kb, vbt, cq, ck, cv, gf, gb, og, bn) = _inproj(
            x, wa, wb, wc, w2f, w2b, gbf, gbb, gq, gk, e64, e32, cos_t, sin_t, pos_block)

        ka_p, ka_s = group_views(ka, A_KW)
        oa = (_attn_a(qat, ka_p, vat, bp, first_p, tm), _attn_a(qat, ka_s, vat, bs, first_s, tm))

        lam_init = 0.8 - 0.6 * math.exp(-0.3 * i)
        lv = b_lambda[i].astype(F32)
        lam = (jnp.exp(jnp.sum(lv[0] * lv[1])) - jnp.exp(jnp.sum(lv[2] * lv[3])) + lam_init).reshape(1, 1)
        gain_col = b_subln[i].reshape(B_V_DIM, 1)
        kb_p, kb_s = group_views(kb, B_QW)
        bn_p, bn_s = bn[:np_tiles * 8], bn[np_tiles * 8:]
        ob = (_attn_b(_alibi_window(bn_p, bp, tp, min(ATT_TQ, tp), tm), lam, qbt, kb_p, vbt, tpos(tm), gain_col,
                      1.0 - lam_init, bp, first_p, tm),
              _attn_b(_alibi_window(bn_s, bs, ts, min(ATT_TQ, ts), tm), lam, qbt, kb_s, vbt, tpos(tm), gain_col,
                      1.0 - lam_init, bs, first_s, tm))

        cq_p, cq_s = group_views(cq, C_KP)
        ck_p, ck_s = group_views(ck, C_KP)
        cv_p, cv_s = group_views(cv, C_VW)
        gf_p, gf_s = group_views(gf, C_KP)
        gb_p, gb_s = group_views(gb, C_KP)
        ocf_p, ocb_p = _gla(cq_p, ck_p, cv_p, gf_p, gb_p, bp, first_p)
        ocf_s, ocb_s = _gla(cq_s, ck_s, cv_s, gf_s, gb_s, bs, first_s)
        ocf = (ocf_p, ocf_s)
        ocb = (ocb_p, ocb_s)

        wo = w_out[i].astype(BF16)
        wr = jnp.zeros((ROUTE_ROWS, D_MODEL), F32)
        wr = wr.at[0:N_GROUPS].set(w_router_coarse[i].T).at[8:].set(w_router_fine[i].T)
        wrh = wr.astype(BF16)
        wrl = (wr - wrh.astype(F32)).astype(BF16)
        br = jnp.zeros((ROUTE_ROWS, 1), F32)
        br = br.at[0:N_GROUPS, 0].set(b_router_coarse[i].astype(F32)).at[8:, 0].set(b_router_fine[i].astype(F32))
        x1, x1t, eid, gate, hist = _outproj(
            x, oa, ob, ocf, ocb, og, jnp.tile(c_norm[i], C_HEADS)[None, :], e96,
            wo[:A_QW], wo[A_QW:A_QW + B_VW], wo[A_QW + B_VW:], ln1_g[i][None, :], ln1_b[i][None, :],
            wrh, wrl, br, alpha)

        block_expert, slot_tok, slot_dst, slot_gate = _route_slots(eid, gate, hist, n_tok)
        y2 = _moe(block_expert, slot_tok, slot_dst, slot_gate, x1t, w_exp_gate[i].astype(BF16),
                  w_exp_up[i].astype(BF16), w_exp_down[i].astype(BF16), slot_gate.shape[0])

        p_all = jnp.concatenate([p_prompt[i].reshape(n_p, PLE_DIM), p_sample[i].reshape(n_s, PLE_DIM)], axis=0)
        x = _final(x1, y2, p_all, w_ple_gate[i].astype(BF16), w_ple_proj[i].astype(BF16),
                   ln2_g[i][None, :], ln2_b[i][None, :], alpha)

    return x[:n_p].reshape(bp, tp, D_MODEL), x[n_p:].reshape(bs, ts, D_MODEL)
```

```python
import functools
import math

import numpy as np
import jax
import jax.numpy as jnp
from jax import lax
from jax.experimental import pallas as pl
from jax.experimental.pallas import tpu as pltpu

F32 = jnp.float32
BF16 = jnp.bfloat16

D_MODEL = 1024
GRID_W = 64
HEAD_DIM = 64
NORM_EPS = 1e-6
A_HEADS = 6
A_KV_HEADS = 2
A_GROUP = A_HEADS // A_KV_HEADS
ROPE_THETA = 10000.0
B_HEADS = 4
B_QK_DIM = 32
B_V_DIM = 64
B_MAPS = 2 * B_HEADS
C_HEADS = 4
C_K_DIM = 48
C_V_DIM = 96
C_GATE_RANK = 16
C_GATE_TAU = 16.0
C_CHUNK = 64
C_KW = C_HEADS * C_K_DIM
C_KP = 256
C_VW = C_HEADS * C_V_DIM
N_GROUPS = 4
EXPERTS_PER_GROUP = 8
N_EXPERTS = N_GROUPS * EXPERTS_PER_GROUP
D_EXPERT = 512
PLE_DIM = 256
A_QW = A_HEADS * HEAD_DIM
A_KW = A_KV_HEADS * HEAD_DIM
B_QW = B_HEADS * 2 * B_QK_DIM
B_VW = B_HEADS * B_V_DIM
ROUTE_ROWS = 8 + N_EXPERTS

VMEM_LIMIT = 56 * 1024 * 1024

ROW_TILE = 512
ATT_TQ = 1024
ATT_COL = 256
ATT_KS_A = 512
ATT_KS_B = 512
ATT_VROWS = HEAD_DIM + 16
LOG2E = math.log2(math.e)
GLA_BLOCK = 512
GLA_PAR = 2
MOE_BM = 256


def _cparams(sem):
    return pltpu.CompilerParams(dimension_semantics=sem, vmem_limit_bytes=VMEM_LIMIT)


def _full(shape):
    nd = len(shape)
    return pl.BlockSpec(shape, lambda *_: (0,) * nd)


def _dot(a, b):
    return jnp.dot(a, b, preferred_element_type=F32)


def _dot_nt(a, b):
    return lax.dot_general(a, b, (((1,), (1,)), ((), ())), preferred_element_type=F32)


def _dot_tn(a, b):
    return lax.dot_general(a, b, (((0,), (0,)), ((), ())), preferred_element_type=F32)


def _split2(x):
    hi = x.astype(BF16)
    lo = (x - hi.astype(F32)).astype(BF16)
    return hi, lo


def _split3(x):
    hi = x.astype(BF16)
    r = x - hi.astype(F32)
    mid = r.astype(BF16)
    lo = (r - mid.astype(F32)).astype(BF16)
    return hi, mid, lo


def _seg_mean(sq, e):
    hi, lo = _split2(sq)
    return _dot(hi, e) + _dot(lo, e)


def _block_diag_avg(width, seg):
    idx = np.arange(width) // seg
    return jnp.asarray((idx[:, None] == idx[None, :]).astype(np.float32) / seg, dtype=BF16)


def _rope(x, cos, sin_signed):
    pieces = []
    for c in range(x.shape[1] // 128):
        xc = x[:, c * 128:(c + 1) * 128]
        nxt = pltpu.roll(xc, 127, 1)
        prv = pltpu.roll(xc, 1, 1)
        lane = lax.broadcasted_iota(jnp.int32, xc.shape, 1)
        sw = jnp.where(lane % 2 == 0, nxt, prv)
        pieces.append(xc * cos + sw * sin_signed)
    return pieces[0] if len(pieces) == 1 else jnp.concatenate(pieces, axis=1)


def _store_vt(vt_ref, v, n_heads):
    vt = v.T
    for h in range(n_heads):
        vt_ref[0, h * ATT_VROWS:h * ATT_VROWS + HEAD_DIM, :] = vt[h * HEAD_DIM:(h + 1) * HEAD_DIM, :].astype(BF16)
        vt_ref[0, h * ATT_VROWS + HEAD_DIM:(h + 1) * ATT_VROWS, :] = jnp.ones(
            (ATT_VROWS - HEAD_DIM, vt.shape[1]), BF16)


def _inproj_kernel(x_ref, wa_ref, wb_ref, wc_ref, w2f_ref, w2b_ref, gbf_ref, gbb_ref, gq_ref, gk_ref,
                   e64_ref, e32_ref, cos_ref, sin_ref,
                   qat_ref, ka_ref, vat_ref, qbt_ref, kb_ref, vbt_ref, cq_ref, ck_ref, cv_ref, gf_ref, gb_ref,
                   og_ref, bn_ref):
    xb = x_ref[...].astype(BF16)
    cos = cos_ref[...]
    sin = sin_ref[...]

    za = _dot(xb, wa_ref[...])
    q = za[:, :A_QW]
    k = za[:, A_QW:A_QW + A_KW]
    e64 = e64_ref[...]
    qn = q * lax.rsqrt(_seg_mean(q * q, e64) + NORM_EPS) * gq_ref[...]
    kn = k * lax.rsqrt(_seg_mean(k * k, e64[:A_KW, :A_KW]) + NORM_EPS) * gk_ref[...]
    qat_ref[...] = (_rope(qn, cos, sin) * (HEAD_DIM ** -0.5 * LOG2E)).T.astype(BF16)
    ka_ref[...] = _rope(kn, cos, sin).astype(BF16)
    _store_vt(vat_ref, za[:, A_QW + A_KW:], A_KV_HEADS)

    zb = _dot(xb, wb_ref[...])
    qb_f = zb[:, :B_QW] * (B_QK_DIM ** -0.5 * LOG2E)
    qb = qb_f.astype(BF16)
    kb = zb[:, B_QW:2 * B_QW].astype(BF16)
    qbt_ref[...] = qb_f.T.astype(BF16)
    kb_ref[...] = kb
    _store_vt(vbt_ref, zb[:, 2 * B_QW:], B_HEADS)

    def max_sqnorm(t):
        tf = t.astype(F32)
        hi, lo = _split2(tf * tf)
        return jnp.max(_dot(hi, e32_ref[...]) + _dot(lo, e32_ref[...]), axis=0, keepdims=True)

    r8 = lax.broadcasted_iota(jnp.int32, (8, 128), 0)
    bn_ref[...] = jnp.where(r8 == 0, max_sqnorm(qb), jnp.where(r8 == 1, max_sqnorm(kb), 0.0))

    zc = _dot(xb, wc_ref[...])
    cq_ref[...] = zc[:, :C_KP] * (C_K_DIM ** -0.5)
    ck_ref[...] = zc[:, C_KP:2 * C_KP]
    cv_ref[...] = zc[:, 2 * C_KP:2 * C_KP + C_VW]
    og = zc[:, 2 * C_KP + C_VW:2 * C_KP + 2 * C_VW]
    og_ref[...] = og * (1.0 / (1.0 + jnp.exp(-og)))
    lr = zc[:, 2 * C_KP + 2 * C_VW:].astype(BF16)

    def log_decay(w2_ref, bias_ref):
        g = _dot(lr, w2_ref[...]) + bias_ref[...]
        return (jnp.minimum(g, 0.0) - jnp.log(1.0 + jnp.exp(-jnp.abs(g)))) * (1.0 / C_GATE_TAU)

    gf_ref[...] = log_decay(w2f_ref, gbf_ref)
    gb_ref[...] = log_decay(w2b_ref, gbb_ref)


def _inproj(x, wa, wb, wc, w2f, w2b, gbf, gbb, gq, gk, e64, e32, cos_t, sin_t, pos_block):
    n = x.shape[0]
    tm = ROW_TILE
    row = lambda w: pl.BlockSpec((tm, w), lambda i: (i, 0))
    tab = pl.BlockSpec((tm, 128), lambda i: (pos_block(i), 0))
    colt = lambda w: (pl.BlockSpec((w, tm), lambda i: (0, i)), jax.ShapeDtypeStruct((w, n), BF16))
    vtile = lambda h: (pl.BlockSpec((1, h * ATT_VROWS, tm), lambda i: (i, 0, 0)),
                       jax.ShapeDtypeStruct((n // tm, h * ATT_VROWS, tm), BF16))
    rowo = lambda w, dt: (row(w), jax.ShapeDtypeStruct((n, w), dt))
    outs = [colt(A_QW), rowo(A_KW, BF16), vtile(A_KV_HEADS), colt(B_QW), rowo(B_QW, BF16), vtile(B_HEADS),
            rowo(C_KP, F32), rowo(C_KP, F32), rowo(C_VW, F32), rowo(C_KP, F32), rowo(C_KP, F32), rowo(C_VW, F32),
            (pl.BlockSpec((8, 128), lambda i: (i, 0)), jax.ShapeDtypeStruct((n // tm * 8, 128), F32))]
    return pl.pallas_call(
        _inproj_kernel,
        grid=(n // tm,),
        in_specs=[row(D_MODEL), _full(wa.shape), _full(wb.shape), _full(wc.shape), _full(w2f.shape),
                  _full(w2b.shape), _full(gbf.shape), _full(gbb.shape), _full(gq.shape), _full(gk.shape),
                  _full(e64.shape), _full(e32.shape), tab, tab],
        out_specs=[s for s, _ in outs],
        out_shape=[t for _, t in outs],
        compiler_params=_cparams(("parallel",)),
        name="inproj",
    )(x, wa, wb, wc, w2f, w2b, gbf, gbb, gq, gk, e64, e32, cos_t, sin_t)


def _attn_loop(k_ref, vt_ref, qpad, s_scr, m_ref, acc_ref, ks, vhead_of_map):
    n_maps, _, tq = qpad.shape
    ncol = tq // ATT_COL
    n_sub = k_ref.shape[1] // ks
    m_ref[...] = jnp.full(m_ref.shape, -jnp.inf, F32)
    acc_ref[...] = jnp.zeros(acc_ref.shape, F32)

    k0 = k_ref[0, 0:ks, :]
    for mp in range(n_maps):
        for c in range(ncol):
            s_scr[mp, c] = _dot(k0, qpad[mp, :, c * ATT_COL:(c + 1) * ATT_COL])

    def key_tile(j, carry):
        jn = jnp.minimum(j + 1, n_sub - 1)
        kn = k_ref[0, pl.ds(pl.multiple_of(jn * ks, ks), ks), :]
        for c in range(ncol):
            cs = slice(c * ATT_COL, (c + 1) * ATT_COL)
            for mp in range(n_maps):
                vh = vhead_of_map[mp]
                s = s_scr[mp, c]
                s_scr[mp, c] = _dot(kn, qpad[mp, :, cs])
                m_prev = m_ref[mp, :, cs]
                m_new = jnp.maximum(m_prev, jnp.max(s, axis=0, keepdims=True))
                p = jnp.exp2(s - m_new).astype(BF16)
                vt = vt_ref[j, vh * ATT_VROWS:(vh + 1) * ATT_VROWS, :]
                acc_ref[mp, :, cs] = jnp.exp2(m_prev - m_new) * acc_ref[mp, :, cs] + _dot(vt, p)
                m_ref[mp, :, cs] = m_new
        return carry

    lax.fori_loop(0, n_sub, key_tile, 0)


def _attn_out(acc_ref, mp):
    return acc_ref[mp, 0:HEAD_DIM, :] / acc_ref[mp, HEAD_DIM:HEAD_DIM + 1, :]


def _attn_a_kernel(qt_ref, k_ref, vt_ref, ot_ref, qpad, s_scr, m_ref, acc_ref, *, ks):
    qpad[...] = jnp.zeros(qpad.shape, BF16)
    for h in range(A_HEADS):
        g = h // A_GROUP
        qpad[h, g * HEAD_DIM:(g + 1) * HEAD_DIM, :] = qt_ref[h * HEAD_DIM:(h + 1) * HEAD_DIM, :]
    _attn_loop(k_ref, vt_ref, qpad, s_scr, m_ref, acc_ref, ks, tuple(h // A_GROUP for h in range(A_HEADS)))
    for h in range(A_HEADS):
        ot_ref[0, h * HEAD_DIM:(h + 1) * HEAD_DIM, :] = _attn_out(acc_ref, h).astype(BF16)


def _attn_scratch(n_maps, qk_width, tq, ks):
    return [pltpu.VMEM((n_maps, qk_width, tq), BF16),
            pltpu.VMEM((n_maps, tq // ATT_COL, ks, ATT_COL), F32),
            pltpu.VMEM((n_maps, 1, tq), F32),
            pltpu.VMEM((n_maps, ATT_VROWS, tq), F32)]


def _attn_a(qt, k, vt, bsz, b0, ks):
    seq = k.shape[1]
    tq = min(ATT_TQ, seq)
    nq = seq // tq
    return pl.pallas_call(
        functools.partial(_attn_a_kernel, ks=ks),
        grid=(bsz, nq),
        in_specs=[pl.BlockSpec((A_QW, tq), lambda b, qi: (0, (b + b0) * nq + qi)),
                  pl.BlockSpec((1, seq, A_KW), lambda b, qi: (b + b0, 0, 0), pipeline_mode=pl.Buffered(1)),
                  pl.BlockSpec((seq // ks, A_KV_HEADS * ATT_VROWS, ks), lambda b, qi: (b + b0, 0, 0),
                               pipeline_mode=pl.Buffered(1))],
        out_specs=pl.BlockSpec((1, A_QW, tq), lambda b, qi: (b, 0, qi)),
        out_shape=jax.ShapeDtypeStruct((bsz, A_QW, seq), BF16),
        scratch_shapes=_attn_scratch(A_HEADS, A_KW, tq, ks),
        compiler_params=_cparams(("parallel", "arbitrary")),
        name="attn_a",
    )(qt, k, vt)


def _alibi_slope(h):
    return 2.0 ** (-8.0 * (h + 1) / B_HEADS)


def _attn_b_kernel(bnd_ref, lam_ref, qt_ref, k_ref, vt_ref, tpos_ref, gain_ref, ot_ref, qpad, s_scr, m_ref,
                   acc_ref, *, ks, out_scale):
    b = pl.program_id(0)
    qi = pl.program_id(1)
    tq = qt_ref.shape[1]
    ncol = tq // ATT_COL
    n_sub = k_ref.shape[1] // ks
    n_diag = tq // ks
    d_lo = qi * n_diag

    qpad[...] = jnp.zeros(qpad.shape, BF16)
    for mp in range(B_MAPS):
        qpad[mp, mp * B_QK_DIM:(mp + 1) * B_QK_DIM, :] = qt_ref[mp * B_QK_DIM:(mp + 1) * B_QK_DIM, :]
    m_ref[...] = jnp.full(m_ref.shape, -jnp.inf, F32)
    acc_ref[...] = jnp.zeros(acc_ref.shape, F32)

    def key_rows(j):
        return k_ref[0, pl.ds(pl.multiple_of(j * ks, ks), ks), :]

    for h in range(B_HEADS):
        slope = _alibi_slope(h) * LOG2E
        base = ((b * pl.num_programs(1) + qi) * B_HEADS + h) * 2
        j_lo = bnd_ref[base]
        j_hi = bnd_ref[base + 1]

        k0 = key_rows(j_lo)
        for mi in range(2):
            for c in range(ncol):
                s_scr[mi, c] = _dot(k0, qpad[2 * h + mi, :, c * ATT_COL:(c + 1) * ATT_COL])

        def key_tile(j, carry, side, h=h, slope=slope):
            kn = key_rows(jnp.minimum(j + 1, n_sub - 1))
            vt = vt_ref[j, h * ATT_VROWS:(h + 1) * ATT_VROWS, :]
            for c in range(ncol):
                cs = slice(c * ATT_COL, (c + 1) * ATT_COL)
                delta = qi * tq + c * ATT_COL - j * ks
                if side == 0:
                    rel = (lax.broadcasted_iota(jnp.int32, (ks, ATT_COL), 1)
                           - lax.broadcasted_iota(jnp.int32, (ks, ATT_COL), 0) + delta)
                    bias = jnp.abs(rel).astype(F32) * (-slope)
                    kappa = 0.0
                else:
                    kappa = delta.astype(F32) * (-side * slope)
                for mi in range(2):
                    mp = 2 * h + mi
                    if side == 0:
                        s = s_scr[mi, c] + bias
                    elif side == 1:
                        s = s_scr[mi, c] - tpos_ref[h]
                    else:
                        s = s_scr[mi, c] + tpos_ref[h]
                    s_scr[mi, c] = _dot(kn, qpad[mp, :, cs])
                    m_prev = m_ref[mp, :, cs]
                    m_new = jnp.maximum(m_prev, jnp.max(s, axis=0, keepdims=True) + kappa)
                    p = jnp.exp2(s - (m_new - kappa)).astype(BF16)
                    acc_ref[mp, :, cs] = jnp.exp2(m_prev - m_new) * acc_ref[mp, :, cs] + _dot(vt, p)
                    m_ref[mp, :, cs] = m_new
            return carry

        lax.fori_loop(j_lo, d_lo, functools.partial(key_tile, side=1), 0)
        for dj in range(n_diag):
            key_tile(d_lo + dj, 0, 0)
        lax.fori_loop(d_lo + n_diag, j_hi, functools.partial(key_tile, side=-1), 0)

    lam = lam_ref[0, 0]
    for h in range(B_HEADS):
        o = _attn_out(acc_ref, 2 * h) - lam * _attn_out(acc_ref, 2 * h + 1)
        ms = jnp.mean(o * o, axis=0, keepdims=True)
        o = o * lax.rsqrt(ms + NORM_EPS) * gain_ref[...] * out_scale
        ot_ref[0, h * B_V_DIM:(h + 1) * B_V_DIM, :] = o.astype(BF16)


def _alibi_window(bn_tiles, bsz, seq, tq, ks):
    nq = seq // tq
    n_sub = seq // ks
    t = bn_tiles.reshape(bsz, nq, -1, 8, 128)
    q2 = t[:, :, :, 0, :B_MAPS].max(axis=2).reshape(bsz, nq, B_HEADS, 2).max(axis=-1)
    k2 = t[:, :, :, 1, :B_MAPS].max(axis=(1, 2)).reshape(bsz, B_HEADS, 2).max(axis=-1)
    u = jnp.sqrt(q2 * k2[:, None, :]) * 1.01
    slope = jnp.asarray([_alibi_slope(h) * LOG2E for h in range(B_HEADS)], F32)
    reach = (2.0 * u + 152.0) / slope
    w = jnp.minimum(jnp.floor((reach + (ks - 1)) / ks), n_sub).astype(jnp.int32)
    d_lo = jnp.arange(nq, dtype=jnp.int32)[None, :, None] * (tq // ks)
    return jnp.stack([jnp.maximum(d_lo - w, 0), jnp.minimum(d_lo + tq // ks + w, n_sub)], axis=-1).reshape(-1)


def _attn_b(bounds, lam, qt, k, vt, tpos, gain_col, out_scale, bsz, b0, ks):
    seq = k.shape[1]
    tq = min(ATT_TQ, seq)
    nq = seq // tq
    assert tq % ks == 0
    grid_spec = pltpu.PrefetchScalarGridSpec(
        num_scalar_prefetch=1,
        grid=(bsz, nq),
        in_specs=[pl.BlockSpec((1, 1), lambda b, qi, bnd: (0, 0), memory_space=pltpu.SMEM),
                  pl.BlockSpec((B_QW, tq), lambda b, qi, bnd: (0, (b + b0) * nq + qi)),
                  pl.BlockSpec((1, seq, B_QW), lambda b, qi, bnd: (b + b0, 0, 0), pipeline_mode=pl.Buffered(1)),
                  pl.BlockSpec((seq // ks, B_HEADS * ATT_VROWS, ks), lambda b, qi, bnd: (b + b0, 0, 0),
                               pipeline_mode=pl.Buffered(1)),
                  pl.BlockSpec((B_HEADS, ks, ATT_COL), lambda b, qi, bnd: (0, 0, 0), pipeline_mode=pl.Buffered(1)),
                  pl.BlockSpec((B_V_DIM, 1), lambda b, qi, bnd: (0, 0))],
        out_specs=pl.BlockSpec((1, B_VW, tq), lambda b, qi, bnd: (b, 0, qi)),
        scratch_shapes=[pltpu.VMEM((B_MAPS, B_QW, tq), BF16),
                        pltpu.VMEM((2, tq // ATT_COL, ks, ATT_COL), F32),
                        pltpu.VMEM((B_MAPS, 1, tq), F32),
                        pltpu.VMEM((B_MAPS, ATT_VROWS, tq), F32)],
    )
    return pl.pallas_call(
        functools.partial(_attn_b_kernel, ks=ks, out_scale=out_scale),
        grid_spec=grid_spec,
        out_shape=jax.ShapeDtypeStruct((bsz, B_VW, seq), BF16),
        compiler_params=_cparams(("parallel", "arbitrary")),
        name="attn_b",
    )(bounds, lam, qt, k, vt, tpos, gain_col)


def _gla_masks():
    j = np.arange(C_CHUNK)
    tri_f = (j[None, :] <= j[:, None]).astype(np.float32)
    tri_b = (j[None, :] >= j[:, None]).astype(np.float32)
    d = np.arange(C_KP)
    dhead = np.where(d < C_KW, d // C_K_DIM, -1)
    hj = np.arange(C_HEADS * C_CHUNK) // C_CHUNK
    vhead = np.arange(C_VW) // C_V_DIM
    m_stack = (hj[:, None] == dhead[None, :]).astype(np.float32)
    m_v = (hj[:, None] == vhead[None, :]).astype(np.float32)
    m_s = (vhead[:, None] == dhead[None, :]).astype(np.float32)
    return (jnp.asarray(np.stack([tri_f, tri_b]), BF16),
            jnp.asarray(np.stack([np.tile(tri_f, (1, C_HEADS)), np.tile(tri_b, (1, C_HEADS))]), F32),
            jnp.asarray(m_stack, F32), jnp.asarray(m_v, F32), jnp.asarray(m_s, F32))


def _gla_kernel(qf_ref, kf_ref, vf_ref, gf_ref, qb_ref, kb_ref, vb_ref, gb_ref, tri_ref, mp_ref, mstack_ref,
                mv_ref, ms_ref, of_ref, ob_ref, st_ref):
    @pl.when(pl.program_id(1) == 0)
    def _():
        st_ref[...] = jnp.zeros(st_ref.shape, F32)

    nb = qf_ref.shape[0]
    n_chunks = qf_ref.shape[1] // C_CHUNK
    chains = [(d, e) for e in range(nb) for d in range(2)]
    srcs = ((qf_ref, kf_ref, vf_ref, gf_ref, of_ref), (qb_ref, kb_ref, vb_ref, gb_ref, ob_ref))

    def chunk(ci, carry):
        rows = [pl.ds(pl.multiple_of((ci if d == 0 else n_chunks - 1 - ci) * C_CHUNK, C_CHUNK), C_CHUNK)
                for d, _ in chains]
        bs = []
        for (d, e), r in zip(chains, rows):
            g_hi, g_mid, g_lo = _split3(srcs[d][3][e, r, :])
            tri = tri_ref[d]
            bs.append(_dot(tri, g_hi) + _dot(tri, g_mid) + _dot(tri, g_lo))
        aa, khats, ps, edges = [], [], [], []
        for (d, e), r, b in zip(chains, rows, bs):
            edge = C_CHUNK - 1 if d == 0 else 0
            b_edge = b[edge:edge + 1, :]
            q = srcs[d][0][e, r, :]
            k = srcs[d][1][e, r, :]
            a = (q * jnp.exp(b)).astype(BF16)
            bm = k * jnp.exp(-b)
            khats.append((k * jnp.exp(b_edge - b)).astype(BF16))
            bm_stack = (jnp.concatenate([bm] * C_HEADS, axis=0) * mstack_ref[...]).astype(BF16)
            ps.append((_dot_nt(a, bm_stack) * mp_ref[d]).astype(BF16))
            aa.append(a)
            edges.append(jnp.exp(b_edge))
        for ix, ((d, e), r) in enumerate(zip(chains, rows)):
            v = srcs[d][2][e, r, :]
            v_bd = (jnp.concatenate([v] * C_HEADS, axis=0) * mv_ref[...]).astype(BF16)
            st = st_ref[ix]
            srcs[d][4][e, r, :] = _dot(ps[ix], v_bd) + _dot_nt(aa[ix], st.astype(BF16))
            st_ref[ix] = st * edges[ix] + _dot_tn(v.astype(BF16), khats[ix]) * ms_ref[...]
        return carry

    lax.fori_loop(0, n_chunks, chunk, 0)


def _gla(q, k, v, gf, gb, bsz, b0):
    seq = q.shape[1]
    blk = min(GLA_BLOCK, seq)
    nblk = seq // blk
    nb = GLA_PAR if bsz % GLA_PAR == 0 and b0 % GLA_PAR == 0 else 1
    masks = _gla_masks()
    fmap = lambda b, t: (b + b0 // nb, t, 0)
    bmap = lambda b, t: (b + b0 // nb, nblk - 1 - t, 0)
    omap = lambda b, t: (b, t, 0)
    ormap = lambda b, t: (b, nblk - 1 - t, 0)
    kw = lambda m: pl.BlockSpec((nb, blk, C_KP), m)
    vw = lambda m: pl.BlockSpec((nb, blk, C_VW), m)
    return pl.pallas_call(
        _gla_kernel,
        grid=(bsz // nb, nblk),
        in_specs=[kw(fmap), kw(fmap), vw(fmap), kw(fmap), kw(bmap), kw(bmap), vw(bmap), kw(bmap)]
                 + [_full(m.shape) for m in masks],
        out_specs=[vw(omap), vw(ormap)],
        out_shape=[jax.ShapeDtypeStruct((bsz, seq, C_VW), F32)] * 2,
        scratch_shapes=[pltpu.VMEM((2 * nb, C_VW, C_KP), F32)],
        compiler_params=_cparams(("parallel", "arbitrary")),
        name="gla",
    )(q, k, v, gf, q, k, v, gb, *masks)


def _layer_norm(x, g, b):
    xc = x - jnp.mean(x, axis=-1, keepdims=True)
    var = jnp.mean(xc * xc, axis=-1, keepdims=True)
    return xc * lax.rsqrt(var + NORM_EPS) * g + b


def _outproj_kernel(x_ref, oap_ref, oas_ref, obp_ref, obs_ref, ocfp_ref, ocfs_ref, ocbp_ref, ocbs_ref, og_ref,
                    cg_ref, e96_ref, woa_ref, wob_ref, woc_ref, g1_ref, b1_ref, wrh_ref, wrl_ref, br_ref,
                    x1_ref, x1t_ref, eid_ref, gate_ref, hist_ref, *, alpha, np_tiles):
    in_prompt = pl.program_id(0) < np_tiles
    oa_t = jnp.where(in_prompt, oap_ref[0], oas_ref[0])
    ob_t = jnp.where(in_prompt, obp_ref[0], obs_ref[0])
    oc = jnp.where(in_prompt, ocfp_ref[0] + ocbp_ref[0], ocfs_ref[0] + ocbs_ref[0])
    oc = oc * lax.rsqrt(_seg_mean(oc * oc, e96_ref[...]) + NORM_EPS) * cg_ref[...] * og_ref[...]
    mixed = (_dot_tn(oa_t, woa_ref[...]) + _dot_tn(ob_t, wob_ref[...])
             + _dot(oc.astype(BF16), woc_ref[...]))
    x1 = _layer_norm(alpha * x_ref[...] + mixed, g1_ref[...], b1_ref[...])
    x1_ref[...] = x1
    for c in range(D_MODEL // 128):
        x1t_ref[pl.ds(c, x1.shape[0], stride=8), :] = x1[:, c * 128:(c + 1) * 128]

    x_hi, x_lo = _split2(x1)
    wrh = wrh_ref[...]
    lt = _dot_nt(wrh, x_hi) + _dot_nt(wrh, x_lo) + _dot_nt(wrl_ref[...], x_hi) + br_ref[...]
    tm = lt.shape[1]
    coarse = lt[0:N_GROUPS]
    r4 = lax.broadcasted_iota(jnp.int32, (N_GROUPS, tm), 0)
    cmax = jnp.max(coarse, axis=0, keepdims=True)
    group = jnp.min(jnp.where(coarse == cmax, r4, N_GROUPS), axis=0, keepdims=True)
    group_w = 1.0 / jnp.sum(jnp.exp(coarse - cmax), axis=0, keepdims=True)
    fine = jnp.zeros((EXPERTS_PER_GROUP, tm), F32)
    for g in range(N_GROUPS):
        fine = jnp.where(group == g, lt[8 + 8 * g:16 + 8 * g], fine)
    r8 = lax.broadcasted_iota(jnp.int32, (EXPERTS_PER_GROUP, tm), 0)
    v1 = jnp.max(fine, axis=0, keepdims=True)
    i1 = jnp.min(jnp.where(fine == v1, r8, EXPERTS_PER_GROUP), axis=0, keepdims=True)
    rest = jnp.where(r8 == i1, -jnp.inf, fine)
    v2 = jnp.max(rest, axis=0, keepdims=True)
    i2 = jnp.min(jnp.where(rest == v2, r8, EXPERTS_PER_GROUP), axis=0, keepdims=True)
    e2 = jnp.exp(v2 - v1)
    w1 = group_w / (1.0 + e2)
    w2 = group_w * e2 / (1.0 + e2)
    e1 = group * EXPERTS_PER_GROUP + i1
    e2nd = group * EXPERTS_PER_GROUP + i2
    eid_ref[...] = jnp.where(r8 == 0, e1, jnp.where(r8 == 1, e2nd, 0))
    gate_ref[...] = jnp.where(r8 == 0, w1, jnp.where(r8 == 1, w2, 0.0))
    r32 = lax.broadcasted_iota(jnp.int32, (N_EXPERTS, tm), 0)
    hits = jnp.where(r32 == e1, 1.0, 0.0) + jnp.where(r32 == e2nd, 1.0, 0.0)
    hist_ref[...] = jnp.broadcast_to(jnp.sum(hits, axis=1, keepdims=True), (N_EXPERTS, 128)).astype(jnp.int32)


def _outproj(x, oa, ob, ocf, ocb, og, cg, e96, woa, wob, woc, g1, b1, wrh, wrl, br, alpha):
    n = x.shape[0]
    tm = ROW_TILE
    np_tiles = oa[0].shape[0] * oa[0].shape[2] // tm
    row = lambda w: pl.BlockSpec((tm, w), lambda i: (i, 0))
    col = pl.BlockSpec((8, tm), lambda i: (0, i))

    def group_specs(arrs, transposed):
        specs = []
        for g, a in enumerate(arrs):
            per_seq = a.shape[2 if transposed else 1] // tm
            last = a.shape[0] * per_seq - 1

            def imap(i, g=g, per_seq=per_seq, last=last):
                t = jnp.clip(i - g * np_tiles, 0, last)
                return (t // per_seq, 0, t % per_seq) if transposed else (t // per_seq, t % per_seq, 0)
            shape = (1, a.shape[1], tm) if transposed else (1, tm, a.shape[2])
            specs.append(pl.BlockSpec(shape, imap))
        return specs

    return pl.pallas_call(
        functools.partial(_outproj_kernel, alpha=alpha, np_tiles=np_tiles),
        grid=(n // tm,),
        in_specs=[row(D_MODEL)] + group_specs(oa, True) + group_specs(ob, True) + group_specs(ocf, False)
                 + group_specs(ocb, False)
                 + [row(C_VW), _full(cg.shape),
                  _full(e96.shape), _full(woa.shape), _full(wob.shape), _full(woc.shape), _full(g1.shape),
                  _full(b1.shape), _full(wrh.shape), _full(wrl.shape), _full(br.shape)],
        out_specs=[row(D_MODEL), pl.BlockSpec((tm * 8, 128), lambda i: (i, 0)), col, col,
                   pl.BlockSpec((N_EXPERTS, 128), lambda i: (i, 0))],
        out_shape=[jax.ShapeDtypeStruct((n, D_MODEL), F32), jax.ShapeDtypeStruct((n * 8, 128), F32),
                   jax.ShapeDtypeStruct((8, n), jnp.int32),
                   jax.ShapeDtypeStruct((8, n), F32),
                   jax.ShapeDtypeStruct((n // tm * N_EXPERTS, 128), jnp.int32)],
        compiler_params=_cparams(("parallel",)),
        name="outproj",
    )(x, *oa, *ob, *ocf, *ocb, og, cg, e96, woa, wob, woc, g1, b1, wrh, wrl, br)


def _moe_kernel(be_ref, tok_ref, tokn_ref, dst_ref, gate_ref, x_hbm, wg_ref, wu_ref, wd_ref, y_hbm,
                xbuf, ybuf, gsem, ssem):
    i = pl.program_id(0)
    nb = pl.num_programs(0)
    bm = xbuf.shape[1] // 8
    slot = i % 2

    def token_rows(t):
        return pl.ds(pl.multiple_of(t * 8, 8), 8)

    def gather(idx_ref, sl, start):
        for r in range(bm):
            cp = pltpu.make_async_copy(x_hbm.at[token_rows(idx_ref[0, 0, r]), :], xbuf.at[sl, pl.ds(r * 8, 8), :],
                                       gsem.at[sl])
            if start:
                cp.start(priority=r % 2)
            else:
                cp.wait()

    def scatter(sl, start):
        for r in range(bm):
            cp = pltpu.make_async_copy(ybuf.at[sl, pl.ds(r * 8, 8), :], y_hbm.at[token_rows(dst_ref[0, 0, r]), :],
                                       ssem.at[sl])
            if start:
                cp.start(priority=r % 2)
            else:
                cp.wait()

    @pl.when(i == 0)
    def _():
        gather(tok_ref, 0, True)

    @pl.when(i >= 2)
    def _():
        scatter(slot, False)

    gather(tok_ref, slot, False)
    gather(tokn_ref, 1 - slot, True)

    xb = jnp.concatenate([xbuf[slot, pl.ds(c, bm, stride=8), :] for c in range(D_MODEL // 128)], axis=1).astype(BF16)
    hg = _dot(xb, wg_ref[0])
    hu = _dot(xb, wu_ref[0])
    h = (hg * (1.0 / (1.0 + jnp.exp(-hg))) * hu).astype(BF16)
    y = _dot(h, wd_ref[0]) * gate_ref[...]
    for c in range(D_MODEL // 128):
        ybuf[slot, pl.ds(c, bm, stride=8), :] = y[:, c * 128:(c + 1) * 128]
    scatter(slot, True)

    @pl.when(i == nb - 1)
    def _():
        scatter(slot, False)
        gather(tok_ref, 1 - slot, False)

        @pl.when(nb >= 2)
        def _():
            scatter(1 - slot, False)


def _moe(block_expert, slot_tok, slot_dst, slot_gate, x1, wg, wu, wd, n_rows_out):
    nb = slot_tok.shape[0]
    bm = MOE_BM
    grid_spec = pltpu.PrefetchScalarGridSpec(
        num_scalar_prefetch=1,
        grid=(nb,),
        in_specs=[pl.BlockSpec((1, 1, bm), lambda i, be: (i, 0, 0), memory_space=pltpu.SMEM),
                  pl.BlockSpec((1, 1, bm), lambda i, be: (jnp.minimum(i + 1, nb - 1), 0, 0),
                               memory_space=pltpu.SMEM),
                  pl.BlockSpec((1, 1, bm), lambda i, be: (i, 0, 0), memory_space=pltpu.SMEM),
                  pl.BlockSpec((bm, 1), lambda i, be: (i, 0)),
                  pl.BlockSpec(memory_space=pl.ANY),
                  pl.BlockSpec((1, D_MODEL, D_EXPERT), lambda i, be: (be[i], 0, 0)),
                  pl.BlockSpec((1, D_MODEL, D_EXPERT), lambda i, be: (be[i], 0, 0)),
                  pl.BlockSpec((1, D_EXPERT, D_MODEL), lambda i, be: (be[i], 0, 0))],
        out_specs=pl.BlockSpec(memory_space=pl.ANY),
        scratch_shapes=[pltpu.VMEM((2, bm * 8, 128), F32), pltpu.VMEM((2, bm * 8, 128), F32),
                        pltpu.SemaphoreType.DMA((2,)), pltpu.SemaphoreType.DMA((2,))],
    )
    return pl.pallas_call(
        _moe_kernel,
        grid_spec=grid_spec,
        out_shape=jax.ShapeDtypeStruct((n_rows_out * 8, 128), F32),
        compiler_params=_cparams(("arbitrary",)),
        name="moe_ffn",
    )(block_expert, slot_tok, slot_tok, slot_dst, slot_gate, x1, wg, wu, wd)


def _route_slots(eid, gate, hist, n_tok):
    bm = MOE_BM
    n_assign = 2 * n_tok
    flat_e = eid[0:2].reshape(-1)
    flat_g = gate[0:2].reshape(-1)
    order = jnp.argsort(flat_e).astype(jnp.int32)
    counts = jnp.sum(hist.reshape(-1, N_EXPERTS, 128)[:, :, 0], axis=0)
    seg_end = jnp.cumsum(counts)
    seg_start = seg_end - counts
    padded = (counts + bm - 1) // bm * bm
    pad_end = jnp.cumsum(padded)
    pad_start = pad_end - padded
    nb = n_assign // bm + N_EXPERTS
    n_slots = nb * bm
    block_start = jnp.arange(nb, dtype=jnp.int32) * bm
    block_expert = jnp.minimum(jnp.sum(block_start[:, None] >= pad_end[None, :], axis=1), N_EXPERTS - 1
                               ).astype(jnp.int32)
    slot = block_start[:, None] + jnp.arange(bm, dtype=jnp.int32)[None, :]
    rank = slot - pad_start[block_expert][:, None]
    valid = rank < counts[block_expert][:, None]
    src = order[jnp.where(valid, seg_start[block_expert][:, None] + rank, 0)]
    slot_tok = jnp.where(valid, src % n_tok, 0)
    slot_dst = jnp.where(valid, src, n_assign + slot - seg_end[block_expert][:, None])
    slot_gate = jnp.where(valid, flat_g[src], 0.0)
    return (block_expert, slot_tok.reshape(nb, 1, bm), slot_dst.reshape(nb, 1, bm), slot_gate.reshape(n_slots, 1))


def _final_kernel(x1_ref, y0_ref, y1_ref, p_ref, wpg_ref, wpp_ref, g2_ref, b2_ref, o_ref, *, alpha):
    x1 = x1_ref[...]
    gate = 1.0 / (1.0 + jnp.exp(-_dot(x1.astype(BF16), wpg_ref[...])))
    ple = gate * _dot(p_ref[...].astype(BF16), wpp_ref[...])
    tm = x1.shape[0]
    ffn = jnp.concatenate([y0_ref[pl.ds(c, tm, stride=8), :] + y1_ref[pl.ds(c, tm, stride=8), :]
                           for c in range(D_MODEL // 128)], axis=1)
    o_ref[...] = _layer_norm(alpha * x1 + ffn + ple, g2_ref[...], b2_ref[...])


def _final(x1, y2, p, wpg, wpp, g2, b2, alpha):
    n = x1.shape[0]
    tm = ROW_TILE
    nt = n // tm
    row = lambda w: pl.BlockSpec((tm, w), lambda i: (i, 0))
    return pl.pallas_call(
        functools.partial(_final_kernel, alpha=alpha),
        grid=(nt,),
        in_specs=[row(D_MODEL), pl.BlockSpec((tm * 8, 128), lambda i: (i, 0)),
                  pl.BlockSpec((tm * 8, 128), lambda i: (i + nt, 0)), row(PLE_DIM),
                  _full(wpg.shape), _full(wpp.shape), _full(g2.shape), _full(b2.shape)],
        out_specs=row(D_MODEL),
        out_shape=jax.ShapeDtypeStruct((n, D_MODEL), F32),
        compiler_params=_cparams(("parallel",)),
        name="final",
    )(x1, y2, y2, p, wpg, wpp, g2, b2)


def _rope_tables(seq_len):
    rows = seq_len // GRID_W
    row = jnp.repeat(jnp.arange(rows, dtype=F32), GRID_W)
    col = jnp.tile(jnp.arange(GRID_W, dtype=F32), rows)
    n_pairs = HEAD_DIM // 4
    inv_freq = ROPE_THETA ** (-jnp.arange(n_pairs, dtype=F32) / n_pairs)
    ang = jnp.concatenate([row[:, None] * inv_freq, col[:, None] * inv_freq], axis=-1)
    cos = jnp.repeat(jnp.cos(ang), 2, axis=-1)
    sin = jnp.repeat(jnp.sin(ang), 2, axis=-1) * jnp.tile(jnp.asarray([-1.0, 1.0], F32), HEAD_DIM // 2)
    return jnp.tile(cos, (1, 2)), jnp.tile(sin, (1, 2))


def _pad_cols(w, width):
    return jnp.pad(w, ((0, 0), (0, width - w.shape[1])))


def kernel(x_prompt, x_sample, p_prompt, p_sample, w_in, a_q_norm, a_k_norm, b_lambda, b_subln, c_gate_w2,
           c_gate_b, c_norm, w_out, ln1_g, ln1_b, w_router_coarse, b_router_coarse, w_router_fine, b_router_fine,
           w_exp_gate, w_exp_up, w_exp_down, w_ple_gate, w_ple_proj, ln2_g, ln2_b):
    depth = w_in.shape[0]
    alpha = (2.0 * depth) ** 0.25
    bp, tp, _ = x_prompt.shape
    bs, ts, _ = x_sample.shape
    n_p, n_s = bp * tp, bs * ts
    n_tok = n_p + n_s
    tm = ROW_TILE
    assert tp % tm == 0 and ts % tm == 0 and tp % GRID_W == 0 and ts % GRID_W == 0
    assert (2 * n_tok) % MOE_BM == 0
    assert n_tok % tp == 0 and n_tok % ts == 0 and n_p % ts == 0
    assert ATT_KS_A == tm and ATT_KS_B == tm

    cos_t, sin_t = _rope_tables(max(tp, ts))
    np_tiles, tp_tiles, ts_tiles = n_p // tm, tp // tm, ts // tm

    def pos_block(i):
        return jnp.where(i < np_tiles, i % tp_tiles, (i - np_tiles) % ts_tiles)

    e64 = _block_diag_avg(A_QW, HEAD_DIM)
    e96 = _block_diag_avg(C_VW, C_V_DIM)
    e32 = jnp.asarray(np.arange(B_QW)[:, None] // B_QK_DIM == np.arange(128)[None, :], BF16)

    def tpos(ks):
        li = (np.arange(ATT_COL)[None, :] - np.arange(ks)[:, None]).astype(np.float32)
        return jnp.stack([jnp.asarray(li) * jnp.float32(_alibi_slope(h) * LOG2E) for h in range(B_HEADS)])
    x = jnp.concatenate([x_prompt.reshape(n_p, D_MODEL), x_sample.reshape(n_s, D_MODEL)], axis=0)

    def group_views(t, width):
        return t.reshape(n_tok // tp, tp, width), t.reshape(n_tok // ts, ts, width)

    first_p, first_s = 0, n_p // ts

    for i in range(depth):
        offs = np.cumsum([0, A_QW, A_KW, A_KW, B_QW, B_QW, B_VW, C_KW, C_KW, C_VW, C_GATE_RANK, C_GATE_RANK, C_VW])
        cols = [w_in[i][:, offs[j]:offs[j + 1]] for j in range(12)]
        wa = jnp.concatenate(cols[0:3], axis=1).astype(BF16)
        wb = jnp.concatenate(cols[3:6], axis=1).astype(BF16)
        wc = jnp.concatenate([_pad_cols(cols[6], C_KP), _pad_cols(cols[7], C_KP), cols[8], cols[11],
                              _pad_cols(jnp.concatenate([cols[9], cols[10]], axis=1), 128)], axis=1).astype(BF16)
        w2 = c_gate_w2[i]
        w2f = jnp.zeros((128, C_KP), F32).at[:C_GATE_RANK, :C_KW].set(w2[0]).astype(BF16)
        w2b = jnp.zeros((128, C_KP), F32).at[C_GATE_RANK:2 * C_GATE_RANK, :C_KW].set(w2[1]).astype(BF16)
        gbf = _pad_cols(c_gate_b[i, 0][None, :], C_KP)
        gbb = _pad_cols(c_gate_b[i, 1][None, :], C_KP)
        gq = jnp.tile(a_q_norm[i], A_HEADS)[None, :]
        gk = jnp.tile(a_k_norm[i], A_KV_HEADS)[None, :]

        (qat, ka, vat, qbt, kb, vbt, cq, ck, cv, gf, gb, og, bn) = _inproj(
            x, wa, wb, wc, w2f, w2b, gbf, gbb, gq, gk, e64, e32, cos_t, sin_t, pos_block)

        ka_p, ka_s = group_views(ka, A_KW)
        oa = (_attn_a(qat, ka_p, vat, bp, first_p, tm), _attn_a(qat, ka_s, vat, bs, first_s, tm))

        lam_init = 0.8 - 0.6 * math.exp(-0.3 * i)
        lv = b_lambda[i].astype(F32)
        lam = (jnp.exp(jnp.sum(lv[0] * lv[1])) - jnp.exp(jnp.sum(lv[2] * lv[3])) + lam_init).reshape(1, 1)
        gain_col = b_subln[i].reshape(B_V_DIM, 1)
        kb_p, kb_s = group_views(kb, B_QW)
        bn_p, bn_s = bn[:np_tiles * 8], bn[np_tiles * 8:]
        ob = (_attn_b(_alibi_window(bn_p, bp, tp, min(ATT_TQ, tp), tm), lam, qbt, kb_p, vbt, tpos(tm), gain_col,
                      1.0 - lam_init, bp, first_p, tm),
              _attn_b(_alibi_window(bn_s, bs, ts, min(ATT_TQ, ts), tm), lam, qbt, kb_s, vbt, tpos(tm), gain_col,
                      1.0 - lam_init, bs, first_s, tm))

        cq_p, cq_s = group_views(cq, C_KP)
        ck_p, ck_s = group_views(ck, C_KP)
        cv_p, cv_s = group_views(cv, C_VW)
        gf_p, gf_s = group_views(gf, C_KP)
        gb_p, gb_s = group_views(gb, C_KP)
        ocf_p, ocb_p = _gla(cq_p, ck_p, cv_p, gf_p, gb_p, bp, first_p)
        ocf_s, ocb_s = _gla(cq_s, ck_s, cv_s, gf_s, gb_s, bs, first_s)
        ocf = (ocf_p, ocf_s)
        ocb = (ocb_p, ocb_s)

        wo = w_out[i].astype(BF16)
        wr = jnp.zeros((ROUTE_ROWS, D_MODEL), F32)
        wr = wr.at[0:N_GROUPS].set(w_router_coarse[i].T).at[8:].set(w_router_fine[i].T)
        wrh = wr.astype(BF16)
        wrl = (wr - wrh.astype(F32)).astype(BF16)
        br = jnp.zeros((ROUTE_ROWS, 1), F32)
        br = br.at[0:N_GROUPS, 0].set(b_router_coarse[i].astype(F32)).at[8:, 0].set(b_router_fine[i].astype(F32))
        x1, x1t, eid, gate, hist = _outproj(
            x, oa, ob, ocf, ocb, og, jnp.tile(c_norm[i], C_HEADS)[None, :], e96,
            wo[:A_QW], wo[A_QW:A_QW + B_VW], wo[A_QW + B_VW:], ln1_g[i][None, :], ln1_b[i][None, :],
            wrh, wrl, br, alpha)

        block_expert, slot_tok, slot_dst, slot_gate = _route_slots(eid, gate, hist, n_tok)
        y2 = _moe(block_expert, slot_tok, slot_dst, slot_gate, x1t, w_exp_gate[i].astype(BF16),
                  w_exp_up[i].astype(BF16), w_exp_down[i].astype(BF16), slot_gate.shape[0])

        p_all = jnp.concatenate([p_prompt[i].reshape(n_p, PLE_DIM), p_sample[i].reshape(n_s, PLE_DIM)], axis=0)
        x = _final(x1, y2, p_all, w_ple_gate[i].astype(BF16), w_ple_proj[i].astype(BF16),
                   ln2_g[i][None, :], ln2_b[i][None, :], alpha)

    return x[:n_p].reshape(bp, tp, D_MODEL), x[n_p:].reshape(bs, ts, D_MODEL)
```
